```python
import math
import jax
import jax.numpy as jnp
from jax import lax
import numpy as np

D_MODEL = 2048
BATCH = 8
SEQ = 2048
DEPTH = 2
DEC_BATCH = 32
DEC_SEQ = 8
PAST_LEN = 8192
PAGE_SIZE = 128

D_MIX = D_MODEL
D_HEAD_ATT = 64
D_ATT = D_MIX // 4
H_ATT = D_ATT // D_HEAD_ATT
Q_BLOCK = 128
D_SSD = D_MIX // 2
P_SSD = 64
H_SSD = D_SSD // P_SSD
G_SSD = 2
E_SSD = H_SSD // G_SSD
N_SSD = 128
SSD_CONV_W = 4
SSD_CONV_DIM = D_SSD + 2 * G_SSD * N_SSD
SSD_CHUNK = 128
C_CONF = D_MIX - D_ATT - D_SSD
CONF_W = 31
D_FF = 128 * ((8 * D_MODEL // 3 + 127) // 128)
FFN_CONV_W = 3
D_IN = 3 * D_ATT + D_SSD + SSD_CONV_DIM + H_SSD + 2 * C_CONF
EPS = 1e-6

kernel_name = "hybrid_stickbreak_ssd_conformer_step"


def _rmsnorm(x, g):
    xf = x.astype(jnp.float32)
    xf = xf * lax.rsqrt(jnp.mean(xf * xf, axis=-1, keepdims=True) + EPS)
    return xf.astype(x.dtype) * g


def _layernorm(x, g, b):
    xf = x.astype(jnp.float32)
    xc = xf - jnp.mean(xf, axis=-1, keepdims=True)
    xf = xc * lax.rsqrt(jnp.mean(xc * xc, axis=-1, keepdims=True) + EPS)
    return xf.astype(x.dtype) * g + b


def _causal_dwconv(x, prev, w, b):
    k = w.shape[0]
    xpad = jnp.concatenate([prev.astype(x.dtype), x], axis=1)
    y = lax.conv_general_dilated(
        xpad, w[:, None, :].astype(x.dtype), window_strides=(1,), padding="VALID",
        dimension_numbers=("NWC", "WIO", "NWC"), feature_group_count=x.shape[-1])
    return y + b, xpad[:, xpad.shape[1] - (k - 1):]


def _sb_block(q, k, v, bias, q_pos, k_pos):
    z = jnp.einsum("bqhd,bkhd->bhqk", q, k, preferred_element_type=jnp.float32) * (D_HEAD_ATT ** -0.5)
    z = z + bias.astype(jnp.float32)[None, :, None, None]
    earlier = k_pos[None, :] < q_pos[:, None]
    log_keep = jnp.where(earlier, jax.nn.log_sigmoid(-z), 0.0)
    later_sum = lax.cumsum(log_keep, axis=3, reverse=True) - log_keep
    w = jnp.where(earlier, jnp.exp(jax.nn.log_sigmoid(z) + later_sum), 0.0)
    return jnp.einsum("bhqk,bkhd->bqhd", w.astype(v.dtype), v)


def _sb_attention(q, k, v, bias, q_pos, k_pos):
    b, tq, h, d = q.shape
    blk = Q_BLOCK if tq % Q_BLOCK == 0 else tq
    nb = tq // blk
    qb = q.reshape(b, nb, blk, h, d).swapaxes(0, 1)
    pb = q_pos.reshape(nb, blk)
    out = lax.map(lambda qp: _sb_block(qp[0], k, v, bias, qp[1], k_pos), (qb, pb))
    return out.swapaxes(0, 1).reshape(b, tq, h, d)


def _segsum(a):
    t = a.shape[-1]
    cs = jnp.cumsum(a, axis=-1)
    diff = cs[..., :, None] - cs[..., None, :]
    return jnp.where(jnp.tril(jnp.ones((t, t), dtype=bool)), diff, -jnp.inf)


def _ssd(x, dt, a_head, bm, cm, h0):
    b, t = x.shape[:2]
    L = SSD_CHUNK if t % SSD_CHUNK == 0 else t
    c = t // L
    xf = (x.astype(jnp.float32) * dt[..., None]).reshape(b, c, L, G_SSD, E_SSD, P_SSD)
    a = (dt * a_head).reshape(b, c, L, G_SSD, E_SSD).transpose(0, 3, 4, 1, 2)
    bc = bm.astype(jnp.float32).reshape(b, c, L, G_SSD, N_SSD)
    cc = cm.astype(jnp.float32).reshape(b, c, L, G_SSD, N_SSD)
    a_cs = jnp.cumsum(a, axis=-1)
    cb = jnp.einsum("bclgn,bcsgn->bgcls", cc, bc)
    mix = cb[:, :, None] * jnp.exp(_segsum(a))
    y_diag = jnp.einsum("bgecls,bcsgep->bclgep", mix, xf)
    decay_to_end = jnp.exp(a_cs[..., -1:] - a_cs)
    chunk_states = jnp.einsum("bclgn,bgecl,bclgep->bcgepn", bc, decay_to_end, xf)
    states = jnp.concatenate(
        [h0.astype(jnp.float32).reshape(b, G_SSD, E_SSD, P_SSD, N_SSD)[:, None], chunk_states], axis=1)
    chunk_tot = jnp.pad(a_cs[..., -1], ((0, 0), (0, 0), (0, 0), (1, 0)))
    states = jnp.einsum("bgezc,bcgepn->bzgepn", jnp.exp(_segsum(chunk_tot)), states)
    h_prev, h_last = states[:, :-1], states[:, -1]
    y_off = jnp.einsum("bclgn,bcgepn,bgecl->bclgep", cc, h_prev, jnp.exp(a_cs))
    y = (y_diag + y_off).reshape(b, t, H_SSD, P_SSD)
    return y.astype(x.dtype), h_last.reshape(b, H_SSD, P_SSD, N_SSD)


def _layer(x, past_k, past_v, ssm0, ssd_conv0, conf_conv0, ffn_conv0, p):
    b, t, _ = x.shape
    pos0 = past_k.shape[1]
    h = _rmsnorm(x, p["norm_mix_g"])
    proj = h @ p["w_in"]
    cuts = np.cumsum([D_ATT, D_ATT, D_ATT, D_SSD, SSD_CONV_DIM, H_SSD]).tolist()
    q, k, v, z, xbc, dt_raw, glu = jnp.split(proj, cuts, axis=-1)

    q = q.reshape(b, t, H_ATT, D_HEAD_ATT)
    k = k.reshape(b, t, H_ATT, D_HEAD_ATT)
    v = v.reshape(b, t, H_ATT, D_HEAD_ATT)
    k_all = jnp.concatenate([past_k.astype(k.dtype), k], axis=1)
    v_all = jnp.concatenate([past_v.astype(v.dtype), v], axis=1)
    att = _sb_attention(q, k_all, v_all, p["att_logit_bias"], pos0 + jnp.arange(t), jnp.arange(pos0 + t))
    att = _rmsnorm(att, p["att_norm_g"].reshape(H_ATT, D_HEAD_ATT)).reshape(b, t, D_ATT)

    xbc, ssd_conv_new = _causal_dwconv(xbc, ssd_conv0, p["ssd_conv_w"], p["ssd_conv_b"])
    xbc = jax.nn.silu(xbc)
    xs, bm, cm = jnp.split(xbc, [D_SSD, D_SSD + G_SSD * N_SSD], axis=-1)
    xs = xs.reshape(b, t, H_SSD, P_SSD)
    dt = jax.nn.softplus(dt_raw.astype(jnp.float32) + p["ssd_dt_bias"])
    a_head = -jnp.exp(p["ssd_a_log"].astype(jnp.float32))
    y, ssm_new = _ssd(xs, dt, a_head, bm.reshape(b, t, G_SSD, N_SSD), cm.reshape(b, t, G_SSD, N_SSD), ssm0)
    y = (y + p["ssd_d"][:, None] * xs).reshape(b, t, D_SSD) * jax.nn.silu(z)
    y = _rmsnorm(y.reshape(b, t, G_SSD, D_SSD // G_SSD),
                 p["ssd_norm_g"].reshape(G_SSD, D_SSD // G_SSD)).reshape(b, t, D_SSD)

    u, u_gate = jnp.split(glu, 2, axis=-1)
    u = u * jax.nn.sigmoid(u_gate)
    u, conf_conv_new = _causal_dwconv(u, conf_conv0, p["conf_conv_w"], p["conf_conv_b"])
    u = jax.nn.silu(_layernorm(u, p["conf_ln_g"], p["conf_ln_b"]))

    x = x + jnp.concatenate([att, y, u], axis=-1) @ p["w_out"]

    h = _rmsnorm(x, p["norm_ffn_g"])
    gate, val = jnp.split(h @ p["w_up"], 2, axis=-1)
    gate, ffn_conv_new = _causal_dwconv(gate, ffn_conv0, p["ffn_conv_w"], p["ffn_conv_b"])
    x = x + (jax.nn.silu(gate) * val) @ p["w_down"]
    return x, (k, v, ssm_new, ssd_conv_new, conf_conv_new, ffn_conv_new)


def setup_inputs(seed: int = 0) -> dict:
    key = jax.random.key(seed)
    ks = jax.random.split(key, 32)
    f32 = jnp.float32
    n_pages = PAST_LEN // PAGE_SIZE
    n_pool = (DEC_BATCH * n_pages * 5) // 4

    def nrm(i, shape, scale):
        return jax.random.normal(ks[i], shape, f32) * scale

    dt0 = jnp.exp(jax.random.uniform(ks[13], (DEPTH, H_SSD), f32, math.log(1e-3), math.log(1e-1)))
    page_table = jax.random.permutation(ks[8], n_pool)[: DEC_BATCH * n_pages]
    att_bias = jnp.linspace(-4.0, -8.0, H_ATT, dtype=f32)[None, :] + nrm(29, (DEPTH, H_ATT), 0.1)
    return {
        "x_prompt": nrm(0, (BATCH, SEQ, D_MODEL), 1.0),
        "x_sample": nrm(1, (DEC_BATCH, DEC_SEQ, D_MODEL), 1.0),
        "cache_k": nrm(2, (DEPTH, n_pool, PAGE_SIZE, H_ATT, D_HEAD_ATT), 1.0),
        "cache_v": nrm(3, (DEPTH, n_pool, PAGE_SIZE, H_ATT, D_HEAD_ATT), 1.0),
        "state_ssm": nrm(4, (DEPTH, DEC_BATCH, H_SSD, P_SSD, N_SSD), 0.1),
        "state_ssd_conv": nrm(5, (DEPTH, DEC_BATCH, SSD_CONV_W - 1, SSD_CONV_DIM), 1.0),
        "state_conf_conv": nrm(6, (DEPTH, DEC_BATCH, CONF_W - 1, C_CONF), 0.5),
        "state_ffn_conv": nrm(7, (DEPTH, DEC_BATCH, FFN_CONV_W - 1, D_FF), 1.0),
        "page_table": page_table.reshape(DEC_BATCH, n_pages).astype(jnp.int32),
        "norm_mix_g": 1.0 + nrm(9, (DEPTH, D_MODEL), 0.02),
        "w_in": nrm(10, (DEPTH, D_MODEL, D_IN), D_MODEL ** -0.5),
        "att_logit_bias": att_bias,
        "att_norm_g": 1.0 + nrm(11, (DEPTH, D_ATT), 0.02),
        "ssd_conv_w": nrm(12, (DEPTH, SSD_CONV_W, SSD_CONV_DIM), SSD_CONV_W ** -0.5),
        "ssd_conv_b": nrm(14, (DEPTH, SSD_CONV_DIM), 0.02),
        "ssd_dt_bias": dt0 + jnp.log(-jnp.expm1(-dt0)),
        "ssd_a_log": jnp.log(jax.random.uniform(ks[15], (DEPTH, H_SSD), f32, 1.0, 16.0)),
        "ssd_d": 1.0 + nrm(16, (DEPTH, H_SSD), 0.02),
        "ssd_norm_g": 1.0 + nrm(17, (DEPTH, D_SSD), 0.02),
        "conf_conv_w": nrm(18, (DEPTH, CONF_W, C_CONF), CONF_W ** -0.5),
        "conf_conv_b": nrm(19, (DEPTH, C_CONF), 0.02),
        "conf_ln_g": 1.0 + nrm(20, (DEPTH, C_CONF), 0.02),
        "conf_ln_b": nrm(21, (DEPTH, C_CONF), 0.02),
        "w_out": nrm(22, (DEPTH, D_MIX, D_MODEL), D_MIX ** -0.5),
        "norm_ffn_g": 1.0 + nrm(23, (DEPTH, D_MODEL), 0.02),
        "w_up": nrm(24, (DEPTH, D_MODEL, 2 * D_FF), D_MODEL ** -0.5),
        "ffn_conv_w": nrm(25, (DEPTH, FFN_CONV_W, D_FF), FFN_CONV_W ** -0.5),
        "ffn_conv_b": nrm(26, (DEPTH, D_FF), 0.02),
        "w_down": nrm(27, (DEPTH, D_FF, D_MODEL), D_FF ** -0.5),
        "norm_final_g": 1.0 + nrm(28, (D_MODEL,), 0.02),
    }


def reference(x_prompt, x_sample, cache_k, cache_v, state_ssm, state_ssd_conv, state_conf_conv,
              state_ffn_conv, page_table, norm_mix_g, w_in, att_logit_bias, att_norm_g, ssd_conv_w,
              ssd_conv_b, ssd_dt_bias, ssd_a_log, ssd_d, ssd_norm_g, conf_conv_w, conf_conv_b, conf_ln_g,
              conf_ln_b, w_out, norm_ffn_g, w_up, ffn_conv_w, ffn_conv_b, w_down, norm_final_g):
    n_pages = PAST_LEN // PAGE_SIZE
    n_seq = page_table.shape[0]
    xp, xs = x_prompt, x_sample
    bp = xp.shape[0]
    st_p, st_s = [], []
    for l in range(DEPTH):
        p = {
            "norm_mix_g": norm_mix_g[l], "w_in": w_in[l], "att_logit_bias": att_logit_bias[l],
            "att_norm_g": att_norm_g[l],
            "ssd_conv_w": ssd_conv_w[l], "ssd_conv_b": ssd_conv_b[l], "ssd_dt_bias": ssd_dt_bias[l],
            "ssd_a_log": ssd_a_log[l], "ssd_d": ssd_d[l], "ssd_norm_g": ssd_norm_g[l],
            "conf_conv_w": conf_conv_w[l], "conf_conv_b": conf_conv_b[l], "conf_ln_g": conf_ln_g[l],
            "conf_ln_b": conf_ln_b[l], "w_out": w_out[l], "norm_ffn_g": norm_ffn_g[l],
            "w_up": w_up[l], "ffn_conv_w": ffn_conv_w[l], "ffn_conv_b": ffn_conv_b[l], "w_down": w_down[l],
        }
        empty_kv = jnp.zeros((bp, 0, H_ATT, D_HEAD_ATT), xp.dtype)
        xp, sp = _layer(
            xp, empty_kv, empty_kv,
            jnp.zeros((bp, H_SSD, P_SSD, N_SSD), jnp.float32),
            jnp.zeros((bp, SSD_CONV_W - 1, SSD_CONV_DIM), xp.dtype),
            jnp.zeros((bp, CONF_W - 1, C_CONF), xp.dtype),
            jnp.zeros((bp, FFN_CONV_W - 1, D_FF), xp.dtype), p)
        st_p.append(sp)
        past_k = cache_k[l][page_table].reshape(n_seq, n_pages * PAGE_SIZE, H_ATT, D_HEAD_ATT)
        past_v = cache_v[l][page_table].reshape(n_seq, n_pages * PAGE_SIZE, H_ATT, D_HEAD_ATT)
        xs, ss = _layer(xs, past_k, past_v, state_ssm[l], state_ssd_conv[l], state_conf_conv[l],
                        state_ffn_conv[l], p)
        st_s.append(ss)
    y_prompt = _rmsnorm(xp, norm_final_g)
    y_sample = _rmsnorm(xs, norm_final_g)
    k_prompt = jnp.stack([s[0] for s in st_p])
    v_prompt = jnp.stack([s[1] for s in st_p])
    k_sample = jnp.stack([s[0] for s in st_s])
    v_sample = jnp.stack([s[1] for s in st_s])
    ssm_prompt = jnp.stack([s[2] for s in st_p])
    ssm_sample = jnp.stack([s[2] for s in st_s])
    ssd_conv_prompt = jnp.stack([s[3] for s in st_p])
    ssd_conv_sample = jnp.stack([s[3] for s in st_s])
    conf_conv_prompt = jnp.stack([s[4] for s in st_p])
    conf_conv_sample = jnp.stack([s[4] for s in st_s])
    ffn_conv_prompt = jnp.stack([s[5] for s in st_p])
    ffn_conv_sample = jnp.stack([s[5] for s in st_s])
    return (y_prompt, y_sample, k_prompt, v_prompt, k_sample, v_sample, ssm_prompt, ssm_sample,
            ssd_conv_prompt, ssd_conv_sample, conf_conv_prompt, conf_conv_sample,
            ffn_conv_prompt, ffn_conv_sample)
```

```python
import functools
import math

import jax
import jax.numpy as jnp
from jax import lax
from jax.experimental import pallas as pl
from jax.experimental.pallas import tpu as pltpu

F32 = jnp.float32
BF16 = jnp.bfloat16
SDS = jax.ShapeDtypeStruct

D_MODEL = 2048
D_HEAD_ATT = 64
D_ATT = 512
H_ATT = 8
D_SSD = 1024
P_SSD = 64
H_SSD = 16
G_SSD = 2
N_SSD = 128
SSD_CONV_W = 4
SSD_CONV_DIM = D_SSD + 2 * G_SSD * N_SSD
SSD_CHUNK = 128
C_CONF = 512
CONF_W = 31
D_FF = 5504
FFN_CONV_W = 3
PAGE_SIZE = 128
EPS = 1e-6

LANES = 128
SUBLANES = 8
VMEM_BYTES_V7X = 64 * 1024 * 1024
VMEM_LIMIT_MAX = VMEM_BYTES_V7X - 8 * 1024 * 1024

D_PROJ = 5120
COL_Z, COL_XS, COL_GLU, COL_BC, COL_Q, COL_K, COL_V = 0, 1024, 2048, 3072, 3584, 4096, 4608
D_FF_PAD = 5632
CONF_PAD = 32
TK_ATT = 128


def _nbytes(shape, dtype):
    return math.prod(shape) * jnp.dtype(dtype).itemsize


def _cparams(sem, bufs):
    est = sum(_nbytes(s, d) * n for s, d, n in bufs)
    return pltpu.CompilerParams(dimension_semantics=sem,
                                vmem_limit_bytes=int(min(max(2 * est, 32 * 1024 * 1024), VMEM_LIMIT_MAX)))


def _tile(m, pref):
    t = min(m, pref)
    assert m % t == 0, (m, pref)
    return t


def _sigmoid(x):
    return 1.0 / (1.0 + jnp.exp(-x))


def _softplus(x):
    return jnp.maximum(x, 0.0) + jnp.log1p(jnp.exp(-jnp.abs(x)))


def _split3(x):
    a = x.astype(BF16)
    r = x - a.astype(F32)
    b = r.astype(BF16)
    c = (r - b.astype(F32)).astype(BF16)
    return a, b, c


def _inproj_body(x_ref, g_ref, w_ref, wdt_ref, o_ref, dt_ref, hn_ref):
    @pl.when(pl.program_id(1) == 0)
    def _():
        x = x_ref[...]
        ms = jnp.mean(x * x, axis=-1, keepdims=True)
        hn_ref[...] = ((x * lax.rsqrt(ms + EPS)) * g_ref[...]).astype(BF16)
        dt_ref[...] = jnp.dot(hn_ref[...], wdt_ref[...], preferred_element_type=F32)

    o_ref[...] = jnp.dot(hn_ref[...], w_ref[...], preferred_element_type=F32)


def _in_proj(x, g, w, wdt):
    m, d = x.shape
    n = w.shape[1]
    tm, tn = _tile(m, 1024), _tile(n, 512)
    return pl.pallas_call(
        _inproj_body,
        grid=(m // tm, n // tn),
        in_specs=[pl.BlockSpec((tm, d), lambda i, j: (i, 0)),
                  pl.BlockSpec((1, d), lambda i, j: (0, 0)),
                  pl.BlockSpec((d, tn), lambda i, j: (0, j)),
                  pl.BlockSpec((d, LANES), lambda i, j: (0, 0))],
        out_specs=[pl.BlockSpec((tm, tn), lambda i, j: (i, j)),
                   pl.BlockSpec((tm, LANES), lambda i, j: (i, 0))],
        out_shape=[SDS((m, n), F32), SDS((m, LANES), F32)],
        scratch_shapes=[pltpu.VMEM((tm, d), BF16)],
        compiler_params=_cparams(("parallel", "arbitrary"),
                                 [((tm, d), F32, 3), ((tm, d), BF16, 1), ((d, tn), BF16, 2),
                                  ((tm, tn), F32, 3), ((tm, LANES), F32, 2), ((d, LANES), BF16, 2)]),
        name="in_proj",
    )(x, g, w, wdt)


def _sb_update(qm, kblk, vblk, bias, uo, carry, acc, mask):
    z = lax.dot_general(qm, kblk, (((1,), (1,)), ((), ())), preferred_element_type=F32) + bias
    lk = -_softplus(z)
    if mask is not None:
        lk = jnp.where(mask, lk, 0.0)
    hi = lk.astype(BF16)
    lo = (lk - hi.astype(F32)).astype(BF16)
    r = jnp.dot(hi, uo, preferred_element_type=F32) + jnp.dot(lo, uo, preferred_element_type=F32)
    w = jnp.exp(z + lk + r[:, :TK_ATT] + carry)
    if mask is not None:
        w = jnp.where(mask, w, 0.0)
    acc = acc + jnp.dot(w.astype(BF16), vblk, preferred_element_type=F32)
    return carry + r[:, TK_ATT:], acc


def _cumsum_matrix():
    jp = lax.broadcasted_iota(jnp.int32, (TK_ATT, 2 * TK_ATT), 0)
    j = lax.broadcasted_iota(jnp.int32, (TK_ATT, 2 * TK_ATT), 1)
    return jnp.where((jp > j) | (j >= TK_ATT), 1.0, 0.0).astype(BF16)


def _att_prompt_body(bias_ref, q_ref, k_ref, v_ref, g_ref, uo_ref, o_ref, *, tq, scale):
    hp = pl.program_id(1)
    qi = pl.program_id(2)
    ndiag = tq // TK_ATT
    lane = lax.broadcasted_iota(jnp.int32, (1, LANES), 1)
    low = lane < D_HEAD_ATT
    uo = uo_ref[...]
    q = q_ref[...] * scale
    qpos = qi * tq + lax.broadcasted_iota(jnp.int32, (tq, 1), 0)
    kiota = lax.broadcasted_iota(jnp.int32, (1, TK_ATT), 1)
    outs = []
    for hh in range(2):
        qm = jnp.where(low if hh == 0 else jnp.logical_not(low), q, 0.0).astype(BF16)
        bias = bias_ref[2 * hp + hh]

        def step(kb, ca, masked, qm=qm, bias=bias):
            k0 = pl.multiple_of(kb * TK_ATT, TK_ATT)
            kblk = k_ref[pl.ds(k0, TK_ATT), :].astype(BF16)
            vblk = v_ref[pl.ds(k0, TK_ATT), :].astype(BF16)
            mask = (k0 + kiota < qpos) if masked else None
            return _sb_update(qm, kblk, vblk, bias, uo, ca[0], ca[1], mask)

        ca = (jnp.zeros((tq, TK_ATT), F32), jnp.zeros((tq, LANES), F32))
        for d in reversed(range(ndiag)):
            ca = step(qi * ndiag + d, ca, True)
        noff = qi * ndiag
        ca = lax.fori_loop(0, noff, lambda t, c, step=step, noff=noff: step(noff - 1 - t, c, False), ca)
        outs.append(ca[1])
    o = jnp.where(low, outs[0], outs[1])
    sq = o * o
    s0 = jnp.sum(jnp.where(low, sq, 0.0), axis=-1, keepdims=True)
    s1 = jnp.sum(jnp.where(low, 0.0, sq), axis=-1, keepdims=True)
    ms = jnp.where(low, s0, s1) * (1.0 / D_HEAD_ATT)
    o_ref[...] = ((o * lax.rsqrt(ms + EPS)) * g_ref[...]).astype(o_ref.dtype)


def _att_prompt(proj, bias, g, uo, nb, seq):
    m = proj.shape[0]
    tq = _tile(seq, 256)
    nq = seq // tq
    npair = H_ATT // 2
    body = functools.partial(_att_prompt_body, tq=tq, scale=D_HEAD_ATT ** -0.5)
    return pl.pallas_call(
        body,
        grid=(nb, npair, nq),
        in_specs=[pl.BlockSpec(memory_space=pltpu.SMEM),
                  pl.BlockSpec((tq, LANES), lambda b, h, i: (b * nq + i, COL_Q // LANES + h)),
                  pl.BlockSpec((seq, LANES), lambda b, h, i: (b, COL_K // LANES + h)),
                  pl.BlockSpec((seq, LANES), lambda b, h, i: (b, COL_V // LANES + h)),
                  pl.BlockSpec((1, LANES), lambda b, h, i: (0, h)),
                  pl.BlockSpec((TK_ATT, 2 * TK_ATT), lambda b, h, i: (0, 0))],
        out_specs=pl.BlockSpec((tq, LANES), lambda b, h, i: (b * nq + i, h)),
        out_shape=SDS((m, D_ATT), BF16),
        compiler_params=_cparams(("parallel", "parallel", "arbitrary"),
                                 [((seq, LANES), F32, 4), ((tq, LANES), F32, 16)]),
        name="att_prompt",
    )(bias, proj, proj, proj, g, uo)


def _att_sample_body(pt_ref, q_ref, kn_ref, vn_ref, bias_ref, g_ref, uo_ref, *rest, npg, scale):
    k_refs, v_refs = rest[:npg], rest[npg:2 * npg]
    o_ref, qbd_ref, kpad_ref, vpad_ref, carry_ref, acc_ref = rest[2 * npg:]
    del pt_ref
    p = pl.program_id(1)
    nrow = H_ATT * SUBLANES
    row = lax.broadcasted_iota(jnp.int32, (nrow, D_ATT), 0)
    lane = lax.broadcasted_iota(jnp.int32, (nrow, D_ATT), 1)
    own = (row >> 3) == (lane >> 6)
    uo = uo_ref[...]
    bias = bias_ref[...]

    @pl.when(p == 0)
    def _():
        q = q_ref[...] * scale
        qt = jnp.concatenate([q] * H_ATT, axis=0)
        qbd = jnp.where(own, qt, 0.0).astype(BF16)
        qbd_ref[...] = qbd
        kpad_ref[...] = jnp.zeros_like(kpad_ref)
        vpad_ref[...] = jnp.zeros_like(vpad_ref)
        kpad_ref[0:SUBLANES, :] = kn_ref[...]
        vpad_ref[0:SUBLANES, :] = vn_ref[...]
        tok = lax.broadcasted_iota(jnp.int32, (nrow, 1), 0) & (SUBLANES - 1)
        mask = lax.broadcasted_iota(jnp.int32, (1, TK_ATT), 1) < tok
        carry, acc = _sb_update(qbd, kpad_ref[...].astype(BF16), vpad_ref[...].astype(BF16), bias, uo,
                                jnp.zeros((nrow, TK_ATT), F32), jnp.zeros((nrow, D_ATT), F32), mask)
        carry_ref[...] = carry
        acc_ref[...] = acc

    carry, acc = carry_ref[...], acc_ref[...]
    qbd = qbd_ref[...]
    for gi in range(npg):
        carry, acc = _sb_update(qbd, k_refs[gi][0].astype(BF16), v_refs[gi][0].astype(BF16), bias, uo,
                                carry, acc, None)
    carry_ref[...] = carry
    acc_ref[...] = acc

    @pl.when(p == pl.num_programs(1) - 1)
    def _():
        a = jnp.where(own, acc, 0.0)
        ms = jnp.sum(a * a, axis=-1, keepdims=True) * (1.0 / D_HEAD_ATT)
        a = a * lax.rsqrt(ms + EPS)
        o = a[0:SUBLANES]
        for h in range(1, H_ATT):
            o = o + a[h * SUBLANES:(h + 1) * SUBLANES]
        o_ref[...] = o * g_ref[...]


def _att_sample(proj, cache_k, cache_v, page_table, layer, n_pool, bias_rows, g, uo):
    m = proj.shape[0]
    nseq, npages = page_table.shape
    assert m == nseq * SUBLANES
    npg = 4 if npages % 4 == 0 else 1
    pt = page_table.reshape(-1)
    ck = cache_k.reshape(-1, PAGE_SIZE, D_ATT)
    cv = cache_v.reshape(-1, PAGE_SIZE, D_ATT)
    nrow = H_ATT * SUBLANES

    def page_map(gi):
        def f(b, p, pt_ref):
            return (layer * n_pool + pt_ref[b * npages + (npages - 1 - (p * npg + gi))], 0, 0)
        return f

    page_specs = [pl.BlockSpec((1, PAGE_SIZE, D_ATT), page_map(gi)) for gi in range(npg)]
    body = functools.partial(_att_sample_body, npg=npg, scale=D_HEAD_ATT ** -0.5)
    grid_spec = pltpu.PrefetchScalarGridSpec(
        num_scalar_prefetch=1,
        grid=(nseq, npages // npg),
        in_specs=[pl.BlockSpec((SUBLANES, D_ATT), lambda b, p, pt_ref: (b, COL_Q // D_ATT)),
                  pl.BlockSpec((SUBLANES, D_ATT), lambda b, p, pt_ref: (b, COL_K // D_ATT)),
                  pl.BlockSpec((SUBLANES, D_ATT), lambda b, p, pt_ref: (b, COL_V // D_ATT)),
                  pl.BlockSpec((nrow, TK_ATT), lambda b, p, pt_ref: (0, 0)),
                  pl.BlockSpec((1, D_ATT), lambda b, p, pt_ref: (0, 0)),
                  pl.BlockSpec((TK_ATT, 2 * TK_ATT), lambda b, p, pt_ref: (0, 0))] + page_specs + page_specs,
        out_specs=pl.BlockSpec((SUBLANES, D_ATT), lambda b, p, pt_ref: (b, 0)),
        scratch_shapes=[pltpu.VMEM((nrow, D_ATT), BF16),
                        pltpu.VMEM((TK_ATT, D_ATT), F32),
                        pltpu.VMEM((TK_ATT, D_ATT), F32),
                        pltpu.VMEM((nrow, TK_ATT), F32),
                        pltpu.VMEM((nrow, D_ATT), F32)])
    return pl.pallas_call(
        body,
        grid_spec=grid_spec,
        out_shape=SDS((m, D_ATT), F32),
        compiler_params=_cparams(("parallel", "arbitrary"),
                                 [((PAGE_SIZE, D_ATT), F32, 4 * npg), ((TK_ATT, D_ATT), F32, 4)]),
        name="att_sample",
    )(pt, proj, proj, proj, bias_rows, g, uo, *([ck] * npg), *([cv] * npg))


def _ssd_body(*refs, lr, nc, has_h0):
    if has_h0:
        (z_ref, xs_ref, bc_ref, dt_ref, c0_ref, h0_ref, cw_ref, cb_ref, dtb_ref, alog_ref, dd_ref, ng_ref,
         tri_ref, y_ref, ht_ref, win_ref, hs_ref, dts_ref, ysc_ref, xd_ref, et_ref) = refs
    else:
        (z_ref, xs_ref, bc_ref, dt_ref, c0_ref, cw_ref, cb_ref, dtb_ref, alog_ref, dd_ref, ng_ref,
         tri_ref, y_ref, ht_ref, win_ref, hs_ref, dts_ref, ysc_ref, xd_ref, et_ref) = refs
        h0_ref = None
    L = SSD_CHUNK
    c = pl.program_id(1)
    hist = SUBLANES

    @pl.when(c == 0)
    def _():
        win_ref[0:hist, :] = c0_ref[0]
        if lr < L:
            win_ref[hist:hist + L, :] = jnp.zeros((L, SSD_CONV_DIM), F32)
            dts_ref[...] = jnp.zeros_like(dts_ref)
        if has_h0:
            hs_ref[...] = h0_ref[0]
        else:
            hs_ref[...] = jnp.zeros_like(hs_ref)

    win_ref[hist:hist + lr, 0:D_SSD] = xs_ref[...]
    win_ref[hist:hist + lr, D_SSD:SSD_CONV_DIM] = bc_ref[...]
    dts_ref[0:lr, :] = dt_ref[...]

    pre = cb_ref[...]
    for k in range(SSD_CONV_W):
        off = hist - (SSD_CONV_W - 1) + k
        pre = pre + cw_ref[k:k + 1, :] * win_ref[off:off + L, :]
    xbc = pre * _sigmoid(pre)
    win_ref[0:hist, :] = win_ref[L:L + hist, :]

    lane = lax.broadcasted_iota(jnp.int32, (1, LANES), 1)
    low = lane < P_SSD
    rowi = lax.broadcasted_iota(jnp.int32, (L, 1), 0)
    dt = _softplus(dts_ref[...] + dtb_ref[...])
    if lr < L:
        dt = jnp.where(rowi < lr, dt, 0.0)
    a_head = jnp.where(lane < H_SSD, -jnp.exp(alog_ref[...]), 0.0)
    a = dt * a_head
    tri = tri_ref[...]
    a1, a2, a3 = _split3(a)
    a_cs = (jnp.dot(tri, a1, preferred_element_type=F32) + jnp.dot(tri, a2, preferred_element_type=F32)
            + jnp.dot(tri, a3, preferred_element_type=F32))
    a_cs_t = a_cs.T
    a_tot = a_cs[L - 1:L, :]
    causal = rowi >= lax.broadcasted_iota(jnp.int32, (1, L), 1)

    cbs, bts, cgs = [], [], []
    for g in range(G_SSD):
        bg = xbc[:, D_SSD + g * N_SSD:D_SSD + (g + 1) * N_SSD]
        cg = xbc[:, D_SSD + (G_SSD + g) * N_SSD:D_SSD + (G_SSD + g + 1) * N_SSD].astype(BF16)
        cbs.append(lax.dot_general(cg, bg.astype(BF16), (((1,), (1,)), ((), ())), preferred_element_type=F32))
        bts.append(bg.T.astype(BF16))
        cgs.append(cg)

    pairs_per_group = H_SSD // G_SSD // 2
    for hp in range(H_SSD // 2):
        g = hp // pairs_per_group
        cols, dtcols, mats = [], [], []
        for h in (2 * hp, 2 * hp + 1):
            col = jnp.broadcast_to(a_cs[:, h:h + 1], (L, LANES))
            dec = jnp.exp(jnp.where(causal, col - a_cs_t[h:h + 1, :], -1e30))
            mats.append((cbs[g] * dec).astype(BF16))
            cols.append(col)
            dtcols.append(jnp.broadcast_to(dt[:, h:h + 1], (L, LANES)))
        sl = slice(hp * LANES, (hp + 1) * LANES)
        xs_pair = xbc[:, sl]
        xdt = xs_pair * jnp.where(low, dtcols[0], dtcols[1])
        xdt_bf = xdt.astype(BF16)
        ydiag = jnp.where(low, jnp.dot(mats[0], xdt_bf, preferred_element_type=F32),
                          jnp.dot(mats[1], xdt_bf, preferred_element_type=F32))
        cs_pair = jnp.where(low, cols[0], cols[1])
        gsl = slice((hp % pairs_per_group) * LANES, (hp % pairs_per_group + 1) * LANES)
        yoff = jnp.dot(cgs[g], hs_ref[g, :, gsl].astype(BF16), preferred_element_type=F32) * jnp.exp(cs_pair)
        ysc_ref[:, sl] = ydiag + yoff + dd_ref[:, sl] * xs_pair
        tot_pair = jnp.where(low, jnp.broadcast_to(a_tot[:, 2 * hp:2 * hp + 1], (1, LANES)),
                             jnp.broadcast_to(a_tot[:, 2 * hp + 1:2 * hp + 2], (1, LANES)))
        xd_ref[:, sl] = (xdt * jnp.exp(tot_pair - cs_pair)).astype(BF16)
        et_ref[:, sl] = jnp.exp(tot_pair)

    gw = D_SSD // G_SSD
    for g in range(G_SSD):
        gs = slice(g * gw, (g + 1) * gw)
        cst = jnp.dot(bts[g], xd_ref[:, gs], preferred_element_type=F32)
        hs_ref[g] = hs_ref[g] * et_ref[:, gs] + cst

    zz = z_ref[...]
    y = ysc_ref[0:lr, :] * (zz * _sigmoid(zz))
    for g in range(G_SSD):
        gs = slice(g * gw, (g + 1) * gw)
        yg = y[:, gs]
        ms = jnp.mean(yg * yg, axis=-1, keepdims=True)
        y_ref[:, gs] = ((yg * lax.rsqrt(ms + EPS)) * ng_ref[:, gs]).astype(y_ref.dtype)

    @pl.when(c == nc - 1)
    def _():
        ht_ref[0] = hs_ref[...]


def _ssd(proj, dtp, conv0, h0, cw, cb, dtb, alog, dd, ng, tri, nb, seq, out_dtype):
    m = proj.shape[0]
    lr = min(seq, SSD_CHUNK)
    assert seq % lr == 0
    nc = seq // lr
    has_h0 = h0 is not None
    gw = D_SSD // G_SSD
    row = lambda b, c: b * nc + c
    in_specs = [pl.BlockSpec((lr, D_SSD), lambda b, c: (row(b, c), COL_Z // D_SSD)),
                pl.BlockSpec((lr, D_SSD), lambda b, c: (row(b, c), COL_XS // D_SSD)),
                pl.BlockSpec((lr, 2 * G_SSD * N_SSD), lambda b, c: (row(b, c), COL_BC // (2 * G_SSD * N_SSD))),
                pl.BlockSpec((lr, LANES), lambda b, c: (row(b, c), 0)),
                pl.BlockSpec((1, SUBLANES, SSD_CONV_DIM), lambda b, c: (b, 0, 0))]
    args = [proj, proj, proj, dtp, conv0]
    if has_h0:
        in_specs.append(pl.BlockSpec((1, G_SSD, N_SSD, gw), lambda b, c: (b, 0, 0, 0)))
        args.append(h0)
    const = lambda shape: pl.BlockSpec(shape, lambda b, c: (0,) * len(shape))
    in_specs += [const((SSD_CONV_W, SSD_CONV_DIM)), const((1, SSD_CONV_DIM)), const((1, LANES)),
                 const((1, LANES)), const((1, D_SSD)), const((1, D_SSD)), const((SSD_CHUNK, SSD_CHUNK))]
    args += [cw, cb, dtb, alog, dd, ng, tri]
    body = functools.partial(_ssd_body, lr=lr, nc=nc, has_h0=has_h0)
    L = SSD_CHUNK
    return pl.pallas_call(
        body,
        grid=(nb, nc),
        in_specs=in_specs,
        out_specs=[pl.BlockSpec((lr, D_SSD), lambda b, c: (row(b, c), 0)),
                   pl.BlockSpec((1, G_SSD, N_SSD, gw), lambda b, c: (b, 0, 0, 0))],
        out_shape=[SDS((m, D_SSD), out_dtype), SDS((nb, G_SSD, N_SSD, gw), F32)],
        scratch_shapes=[pltpu.VMEM((L + SUBLANES, SSD_CONV_DIM), F32),
                        pltpu.VMEM((G_SSD, N_SSD, gw), F32),
                        pltpu.VMEM((L, LANES), F32),
                        pltpu.VMEM((L, D_SSD), F32),
                        pltpu.VMEM((L, D_SSD), BF16),
                        pltpu.VMEM((1, D_SSD), F32)],
        compiler_params=_cparams(("parallel", "arbitrary"),
                                 [((L, SSD_CONV_DIM), F32, 12), ((G_SSD, N_SSD, gw), F32, 5)]),
        name="ssd",
    )(*args)


def _conf_body(glu_ref, c0_ref, w_ref, b_ref, lg_ref, lb_ref, o_ref, tail_ref, win_ref, *, tt, nt):
    t = pl.program_id(1)

    @pl.when(t == 0)
    def _():
        win_ref[0:CONF_PAD, :] = c0_ref[0]

    glu = glu_ref[...]
    win_ref[CONF_PAD:CONF_PAD + tt, :] = glu[:, :C_CONF] * _sigmoid(glu[:, C_CONF:])
    acc = jnp.broadcast_to(b_ref[...], (tt, C_CONF))
    for k in range(CONF_W):
        off = CONF_PAD - (CONF_W - 1) + k
        acc = acc + w_ref[k:k + 1, :] * win_ref[off:off + tt, :]
    mu = jnp.mean(acc, axis=-1, keepdims=True)
    xc = acc - mu
    var = jnp.mean(xc * xc, axis=-1, keepdims=True)
    yn = (xc * lax.rsqrt(var + EPS)) * lg_ref[...] + lb_ref[...]
    o_ref[...] = (yn * _sigmoid(yn)).astype(o_ref.dtype)
    tail = win_ref[tt:tt + CONF_PAD, :]
    tail_ref[0] = tail
    if nt > 1:
        win_ref[0:CONF_PAD, :] = tail


def _conformer(proj, conv0, w, b, lg, lb, nb, seq, out_dtype):
    m = proj.shape[0]
    tt = _tile(seq, 256)
    nt = seq // tt
    assert nt == 1 or tt >= CONF_PAD
    body = functools.partial(_conf_body, tt=tt, nt=nt)
    const = lambda shape: pl.BlockSpec(shape, lambda bb, t: (0,) * len(shape))
    return pl.pallas_call(
        body,
        grid=(nb, nt),
        in_specs=[pl.BlockSpec((tt, 2 * C_CONF), lambda bb, t: (bb * nt + t, COL_GLU // (2 * C_CONF))),
                  pl.BlockSpec((1, CONF_PAD, C_CONF), lambda bb, t: (bb, 0, 0)),
                  const((CONF_W, C_CONF)), const((1, C_CONF)), const((1, C_CONF)), const((1, C_CONF))],
        out_specs=[pl.BlockSpec((tt, C_CONF), lambda bb, t: (bb * nt + t, 0)),
                   pl.BlockSpec((1, CONF_PAD, C_CONF), lambda bb, t: (bb, 0, 0))],
        out_shape=[SDS((m, C_CONF), out_dtype), SDS((nb, CONF_PAD, C_CONF), F32)],
        scratch_shapes=[pltpu.VMEM((CONF_PAD + tt, C_CONF), F32)],
        compiler_params=_cparams(("parallel", "arbitrary"), [((tt, 2 * C_CONF), F32, 8)]),
        name="conformer",
    )(proj, conv0, w, b, lg, lb)


def _outproj_body(x_ref, a_ref, y_ref, u_ref, w_ref, g_ref, x1_ref, h_ref):
    acc = x_ref[...]
    acc = acc + jnp.dot(a_ref[...].astype(BF16), w_ref[0:D_ATT, :], preferred_element_type=F32)
    acc = acc + jnp.dot(y_ref[...].astype(BF16), w_ref[D_ATT:D_ATT + D_SSD, :], preferred_element_type=F32)
    acc = acc + jnp.dot(u_ref[...].astype(BF16), w_ref[D_ATT + D_SSD:, :], preferred_element_type=F32)
    x1_ref[...] = acc
    ms = jnp.mean(acc * acc, axis=-1, keepdims=True)
    h_ref[...] = ((acc * lax.rsqrt(ms + EPS)) * g_ref[...]).astype(BF16)


def _out_proj(x, att, y, u, w, g):
    m, d = x.shape
    tm = _tile(m, 512)
    row = lambda width: pl.BlockSpec((tm, width), lambda i: (i, 0))
    return pl.pallas_call(
        _outproj_body,
        grid=(m // tm,),
        in_specs=[row(d), row(D_ATT), row(D_SSD), row(C_CONF),
                  pl.BlockSpec((d, d), lambda i: (0, 0)), pl.BlockSpec((1, d), lambda i: (0, 0))],
        out_specs=[row(d), row(d)],
        out_shape=[SDS((m, d), F32), SDS((m, d), BF16)],
        compiler_params=_cparams(("parallel",), [((tm, d), F32, 7), ((d, d), BF16, 2)]),
        name="out_proj",
    )(x, att, y, u, w, g)


HALO = 16


def _ffn_up_body(*refs, tm, seq, sample):
    if sample:
        h_ref, wg_ref, wv_ref, cw_ref, cb_ref, s1_ref, s2_ref, act_ref, tail_ref, lhs_ref, gs_ref = refs
    else:
        h_ref, halo_ref, wg_ref, wv_ref, cw_ref, cb_ref, act_ref, tail_ref, lhs_ref, gs_ref = refs
    i = pl.program_id(0)

    @pl.when(pl.program_id(1) == 0)
    def _():
        lhs_ref[HALO:, :] = h_ref[...]
        if sample:
            lhs_ref[0:HALO, :] = jnp.zeros((HALO, D_MODEL), BF16)
        else:
            lhs_ref[0:HALO, :] = halo_ref[...]

            @pl.when((i * tm) % seq == 0)
            def _():
                lhs_ref[0:HALO, :] = jnp.zeros((HALO, D_MODEL), BF16)

    gs_ref[...] = jnp.dot(lhs_ref[...], wg_ref[...], preferred_element_type=F32)
    val = jnp.dot(lhs_ref[HALO:, :], wv_ref[...], preferred_element_type=F32)
    g0 = gs_ref[HALO:HALO + tm, :]
    g1 = gs_ref[HALO - 1:HALO - 1 + tm, :]
    g2 = gs_ref[HALO - 2:HALO - 2 + tm, :]
    if sample:
        tok = lax.broadcasted_iota(jnp.int32, (tm, 1), 0) & (seq - 1)
        g1 = jnp.where(tok < 1, s1_ref[...], g1)
        g2 = jnp.where(tok < 2, s2_ref[...], g2)
        tail_ref[...] = g0
    else:
        tail_ref[...] = gs_ref[HALO + tm - SUBLANES:HALO + tm, :]
    pre = cw_ref[0:1, :] * g2 + cw_ref[1:2, :] * g1 + cw_ref[2:3, :] * g0 + cb_ref[...]
    act_ref[...] = ((pre * _sigmoid(pre)) * val).astype(BF16)


def _ffn_up(h, wg, wv, cw, cb, seq, s1=None, s2=None):
    m, d = h.shape
    f = wg.shape[1]
    sample = s1 is not None
    tm, tn = _tile(m, m if sample else min(1024, seq)), _tile(f, 512)
    if sample:
        assert seq & (seq - 1) == 0 and seq >= FFN_CONV_W - 1
    else:
        assert seq % tm == 0
    body = functools.partial(_ffn_up_body, tm=tm, seq=seq, sample=sample)
    hspec = pl.BlockSpec((tm, d), lambda i, j: (i, 0))
    wspec = pl.BlockSpec((d, tn), lambda i, j: (0, j))
    cspec = lambda r: pl.BlockSpec((r, tn), lambda i, j: (0, j))
    tile = pl.BlockSpec((tm, tn), lambda i, j: (i, j))
    if sample:
        in_specs = [hspec, wspec, wspec, cspec(FFN_CONV_W), cspec(1), tile, tile]
        args = (h, wg, wv, cw, cb, s1, s2)
        tail_rows, tail_spec = m, tile
    else:
        halo = pl.BlockSpec((HALO, d), lambda i, j: (jnp.maximum(i * (tm // HALO) - 1, 0), 0))
        in_specs = [hspec, halo, wspec, wspec, cspec(FFN_CONV_W), cspec(1)]
        args = (h, h, wg, wv, cw, cb)
        tail_rows, tail_spec = (m // tm) * SUBLANES, pl.BlockSpec((SUBLANES, tn), lambda i, j: (i, j))
    return pl.pallas_call(
        body,
        grid=(m // tm, f // tn),
        in_specs=in_specs,
        out_specs=[tile, tail_spec],
        out_shape=[SDS((m, f), BF16), SDS((tail_rows, f), F32)],
        scratch_shapes=[pltpu.VMEM((HALO + tm, d), BF16), pltpu.VMEM((HALO + tm, tn), F32)],
        compiler_params=_cparams(("parallel", "arbitrary"),
                                 [((tm, d), BF16, 3), ((d, tn), BF16, 4), ((tm, tn), F32, 8)]),
        name="ffn_up",
    )(*args)


def _ffn_down_body(x_ref, a_ref, w_ref, o_ref):
    o_ref[...] = x_ref[...] + jnp.dot(a_ref[...], w_ref[...], preferred_element_type=F32)


def _ffn_down(x, act, w):
    m, d = x.shape
    f = act.shape[1]
    tm, tn = _tile(m, 512), _tile(d, 512)
    return pl.pallas_call(
        _ffn_down_body,
        grid=(m // tm, d // tn),
        in_specs=[pl.BlockSpec((tm, tn), lambda i, j: (i, j)),
                  pl.BlockSpec((tm, f), lambda i, j: (i, 0)),
                  pl.BlockSpec((f, tn), lambda i, j: (0, j))],
        out_specs=pl.BlockSpec((tm, tn), lambda i, j: (i, j)),
        out_shape=SDS((m, d), F32),
        compiler_params=_cparams(("parallel", "arbitrary"),
                                 [((tm, f), BF16, 2), ((f, tn), BF16, 2), ((tm, tn), F32, 6)]),
        name="ffn_down",
    )(x, act, w)


def _final_norm_body(x_ref, g_ref, o_ref):
    x = x_ref[...]
    ms = jnp.mean(x * x, axis=-1, keepdims=True)
    o_ref[...] = (x * lax.rsqrt(ms + EPS)) * g_ref[...]


def _final_norm(x, g):
    m, d = x.shape
    tm = _tile(m, 512)
    return pl.pallas_call(
        _final_norm_body,
        grid=(m // tm,),
        in_specs=[pl.BlockSpec((tm, d), lambda i: (i, 0)), pl.BlockSpec((1, d), lambda i: (0, 0))],
        out_specs=pl.BlockSpec((tm, d), lambda i: (i, 0)),
        out_shape=SDS((m, d), F32),
        compiler_params=_cparams(("parallel",), [((tm, d), F32, 6)]),
        name="final_norm",
    )(x, g)


def _pad_cols(x, width):
    return jnp.pad(x, ((0, 0), (0, width - x.shape[1])))


def _layer_weights(l, p):
    w_in = p["w_in"][l]
    c_z, c_x, c_dt = 3 * D_ATT, 3 * D_ATT + D_SSD, 3 * D_ATT + D_SSD + SSD_CONV_DIM
    c_glu = c_dt + H_SSD
    order = [w_in[:, c_z:c_z + D_SSD],
             w_in[:, c_x:c_x + D_SSD],
             w_in[:, c_glu:c_glu + 2 * C_CONF],
             w_in[:, c_x + D_SSD:c_dt],
             w_in[:, 0:3 * D_ATT]]
    w_up = p["w_up"][l]
    row = lambda v: v.reshape(1, -1)
    return dict(
        w_main=jnp.concatenate(order, axis=1).astype(BF16),
        w_dt=_pad_cols(w_in[:, c_dt:c_dt + H_SSD], LANES).astype(BF16),
        g_mix=row(p["norm_mix_g"][l]),
        att_bias=p["att_logit_bias"][l],
        att_g=row(p["att_norm_g"][l]),
        ssd_cw=p["ssd_conv_w"][l], ssd_cb=row(p["ssd_conv_b"][l]),
        dtb=_pad_cols(row(p["ssd_dt_bias"][l]), LANES), alog=_pad_cols(row(p["ssd_a_log"][l]), LANES),
        dd=row(jnp.repeat(p["ssd_d"][l], P_SSD)), ssd_ng=row(p["ssd_norm_g"][l]),
        conf_w=p["conf_conv_w"][l], conf_b=row(p["conf_conv_b"][l]),
        conf_lg=row(p["conf_ln_g"][l]), conf_lb=row(p["conf_ln_b"][l]),
        w_out=p["w_out"][l].astype(BF16), g_ffn=row(p["norm_ffn_g"][l]),
        w_gate=_pad_cols(w_up[:, :D_FF], D_FF_PAD).astype(BF16),
        w_val=_pad_cols(w_up[:, D_FF:], D_FF_PAD).astype(BF16),
        ffn_cw=_pad_cols(p["ffn_conv_w"][l], D_FF_PAD), ffn_cb=_pad_cols(row(p["ffn_conv_b"][l]), D_FF_PAD),
        w_down=jnp.pad(p["w_down"][l], ((0, D_FF_PAD - D_FF), (0, 0))).astype(BF16),
    )


def _ssm_to_kernel_layout(h):
    b = h.shape[0]
    e = H_SSD // G_SSD
    return h.reshape(b, G_SSD, e, P_SSD, N_SSD).transpose(0, 1, 4, 2, 3).reshape(b, G_SSD, N_SSD, e * P_SSD)


def _ssm_from_kernel_layout(h):
    b = h.shape[0]
    e = H_SSD // G_SSD
    return h.reshape(b, G_SSD, N_SSD, e, P_SSD).transpose(0, 1, 3, 4, 2).reshape(b, H_SSD, P_SSD, N_SSD)


def _xbc_cols(proj3):
    return jnp.concatenate([proj3[..., COL_XS:COL_XS + D_SSD], proj3[..., COL_BC:COL_BC + 2 * G_SSD * N_SSD]],
                           axis=-1)


def kernel(x_prompt, x_sample, cache_k, cache_v, state_ssm, state_ssd_conv, state_conf_conv, state_ffn_conv, page_table, norm_mix_g, w_in, att_logit_bias, att_norm_g, ssd_conv_w, ssd_conv_b, ssd_dt_bias, ssd_a_log, ssd_d, ssd_norm_g, conf_conv_w, conf_conv_b, conf_ln_g, conf_ln_b, w_out, norm_ffn_g, w_up, ffn_conv_w, ffn_conv_b, w_down, norm_final_g):
    params = dict(norm_mix_g=norm_mix_g, w_in=w_in, att_logit_bias=att_logit_bias, att_norm_g=att_norm_g,
                  ssd_conv_w=ssd_conv_w, ssd_conv_b=ssd_conv_b, ssd_dt_bias=ssd_dt_bias, ssd_a_log=ssd_a_log,
                  ssd_d=ssd_d, ssd_norm_g=ssd_norm_g, conf_conv_w=conf_conv_w, conf_conv_b=conf_conv_b,
                  conf_ln_g=conf_ln_g, conf_ln_b=conf_ln_b, w_out=w_out, norm_ffn_g=norm_ffn_g, w_up=w_up,
                  ffn_conv_w=ffn_conv_w, ffn_conv_b=ffn_conv_b, w_down=w_down)
    depth = w_in.shape[0]
    bp, sp, d = x_prompt.shape
    bs, ss, _ = x_sample.shape
    assert ss == SUBLANES and d == D_MODEL
    n_pool = cache_k.shape[1]
    xp = x_prompt.reshape(bp * sp, d)
    xs = x_sample.reshape(bs * ss, d)
    uo = _cumsum_matrix()
    tri = jnp.tril(jnp.ones((SSD_CHUNK, SSD_CHUNK), F32)).astype(BF16)
    zeros_ssd_conv = jnp.zeros((bp, SUBLANES, SSD_CONV_DIM), F32)
    zeros_conf_conv = jnp.zeros((bp, CONF_PAD, C_CONF), F32)
    outs_p, outs_s = [], []
    for l in range(depth):
        w = _layer_weights(l, params)

        proj, dtp = _in_proj(xp, w["g_mix"], w["w_main"], w["w_dt"])
        att = _att_prompt(proj, w["att_bias"], w["att_g"], uo, bp, sp)
        y, h_t = _ssd(proj, dtp, zeros_ssd_conv, None, w["ssd_cw"], w["ssd_cb"], w["dtb"], w["alog"], w["dd"],
                      w["ssd_ng"], tri, bp, sp, BF16)
        u, conf_tail = _conformer(proj, zeros_conf_conv, w["conf_w"], w["conf_b"], w["conf_lg"], w["conf_lb"],
                                  bp, sp, BF16)
        x1, h2 = _out_proj(xp, att, y, u, w["w_out"], w["g_ffn"])
        act, gate_tail = _ffn_up(h2, w["w_gate"], w["w_val"], w["ffn_cw"], w["ffn_cb"], sp)
        xp = _ffn_down(x1, act, w["w_down"])
        proj3 = proj.reshape(bp, sp, D_PROJ)
        tiles_per_seq = gate_tail.shape[0] // SUBLANES // bp
        gt = gate_tail.reshape(bp, tiles_per_seq, SUBLANES, D_FF_PAD)
        outs_p.append((
            proj3[..., COL_K:COL_K + D_ATT].reshape(bp, sp, H_ATT, D_HEAD_ATT),
            proj3[..., COL_V:COL_V + D_ATT].reshape(bp, sp, H_ATT, D_HEAD_ATT),
            _ssm_from_kernel_layout(h_t),
            _xbc_cols(proj3[:, sp - (SSD_CONV_W - 1):]),
            conf_tail[:, CONF_PAD - (CONF_W - 1):],
            gt[:, -1, SUBLANES - (FFN_CONV_W - 1):, :D_FF]))

        proj, dtp = _in_proj(xs, w["g_mix"], w["w_main"], w["w_dt"])
        bias_rows = jnp.broadcast_to(jnp.repeat(w["att_bias"], SUBLANES)[:, None], (H_ATT * SUBLANES, TK_ATT))
        att = _att_sample(proj, cache_k, cache_v, page_table, l, n_pool, bias_rows, w["att_g"], uo)
        conv0 = jnp.pad(state_ssd_conv[l], ((0, 0), (SUBLANES - (SSD_CONV_W - 1), 0), (0, 0)))
        y, h_t = _ssd(proj, dtp, conv0, _ssm_to_kernel_layout(state_ssm[l]), w["ssd_cw"], w["ssd_cb"], w["dtb"],
                      w["alog"], w["dd"], w["ssd_ng"], tri, bs, ss, F32)
        conf0 = jnp.pad(state_conf_conv[l], ((0, 0), (CONF_PAD - (CONF_W - 1), 0), (0, 0)))
        u, conf_tail = _conformer(proj, conf0, w["conf_w"], w["conf_b"], w["conf_lg"], w["conf_lb"], bs, ss, F32)
        x1, h2 = _out_proj(xs, att, y, u, w["w_out"], w["g_ffn"])
        st = _pad_cols(state_ffn_conv[l].reshape(bs * (FFN_CONV_W - 1), D_FF), D_FF_PAD).reshape(
            bs, FFN_CONV_W - 1, D_FF_PAD)
        s1 =jnp.concatenate([st[:, 1:2], jnp.zeros((bs, ss - 1, D_FF_PAD), F32)], axis=1).reshape(bs * ss, -1)
        s2 = jnp.concatenate([st, jnp.zeros((bs, ss - 2, D_FF_PAD), F32)], axis=1).reshape(bs * ss, -1)
        act, gate_full = _ffn_up(h2, w["w_gate"], w["w_val"], w["ffn_cw"], w["ffn_cb"], ss, s1, s2)
        xs = _ffn_down(x1, act, w["w_down"])
        proj3 = proj.reshape(bs, ss, D_PROJ)
        outs_s.append((
            proj3[..., COL_K:COL_K + D_ATT].reshape(bs, ss, H_ATT, D_HEAD_ATT),
            proj3[..., COL_V:COL_V + D_ATT].reshape(bs, ss, H_ATT, D_HEAD_ATT),
            _ssm_from_kernel_layout(h_t),
            _xbc_cols(proj3[:, ss - (SSD_CONV_W - 1):]),
            conf_tail[:, CONF_PAD - (CONF_W - 1):],
            gate_full.reshape(bs, ss, D_FF_PAD)[:, ss - (FFN_CONV_W - 1):, :D_FF]))

    g_final = norm_final_g.reshape(1, d)
    y_prompt = _final_norm(xp, g_final).reshape(bp, sp, d)
    y_sample = _final_norm(xs, g_final).reshape(bs, ss, d)
    stack = lambda outs, i: jnp.stack([o[i] for o in outs])
    return (y_prompt, y_sample,
            stack(outs_p, 0), stack(outs_p, 1), stack(outs_s, 0), stack(outs_s, 1),
            stack(outs_p, 2), stack(outs_s, 2), stack(outs_p, 3), stack(outs_s, 3),
            stack(outs_p, 4), stack(outs_s, 4), stack(outs_p, 5), stack(outs_s, 5))
```

```python
import functools
import math

import jax
import jax.numpy as jnp
from jax import lax
from jax.experimental import pallas as pl
from jax.experimental.pallas import tpu as pltpu

F32 = jnp.float32
BF16 = jnp.bfloat16
SDS = jax.ShapeDtypeStruct

D_MODEL = 2048
D_HEAD_ATT = 64
D_ATT = 512
H_ATT = 8
D_SSD = 1024
P_SSD = 64
H_SSD = 16
G_SSD = 2
N_SSD = 128
SSD_CONV_W = 4
SSD_CONV_DIM = D_SSD + 2 * G_SSD * N_SSD
SSD_CHUNK = 128
C_CONF = 512
CONF_W = 31
D_FF = 5504
FFN_CONV_W = 3
PAGE_SIZE = 128
EPS = 1e-6

LANES = 128
SUBLANES = 8
VMEM_BYTES_V7X = 64 * 1024 * 1024
VMEM_LIMIT_MAX = VMEM_BYTES_V7X - 8 * 1024 * 1024

D_PROJ = 5120
COL_Z, COL_XS, COL_GLU, COL_BC, COL_Q, COL_K, COL_V = 0, 1024, 2048, 3072, 3584, 4096, 4608
D_FF_PAD = 5632
CONF_PAD = 32
TK_ATT = 128


def _nbytes(shape, dtype):
    return math.prod(shape) * jnp.dtype(dtype).itemsize


def _cparams(sem, bufs):
    est = sum(_nbytes(s, d) * n for s, d, n in bufs)
    return pltpu.CompilerParams(dimension_semantics=sem,
                                vmem_limit_bytes=int(min(max(2 * est, 32 * 1024 * 1024), VMEM_LIMIT_MAX)))


def _tile(m, pref):
    t = min(m, pref)
    assert m % t == 0, (m, pref)
    return t


def _sigmoid(x):
    return 1.0 / (1.0 + jnp.exp(-x))


def _softplus(x):
    return jnp.maximum(x, 0.0) + jnp.log1p(jnp.exp(-jnp.abs(x)))


def _split3(x):
    a = x.astype(BF16)
    r = x - a.astype(F32)
    b = r.astype(BF16)
    c = (r - b.astype(F32)).astype(BF16)
    return a, b, c


def _inproj_body(x_ref, g_ref, w_ref, wdt_ref, o_ref, dt_ref, hn_ref):
    @pl.when(pl.program_id(1) == 0)
    def _():
        x = x_ref[...]
        ms = jnp.mean(x * x, axis=-1, keepdims=True)
        hn_ref[...] = ((x * lax.rsqrt(ms + EPS)) * g_ref[...]).astype(BF16)
        dt_ref[...] = jnp.dot(hn_ref[...], wdt_ref[...], preferred_element_type=F32)

    o_ref[...] = jnp.dot(hn_ref[...], w_ref[...], preferred_element_type=F32)


def _in_proj(x, g, w, wdt):
    m, d = x.shape
    n = w.shape[1]
    tm, tn = _tile(m, 1024), _tile(n, 512)
    return pl.pallas_call(
        _inproj_body,
        grid=(m // tm, n // tn),
        in_specs=[pl.BlockSpec((tm, d), lambda i, j: (i, 0)),
                  pl.BlockSpec((1, d), lambda i, j: (0, 0)),
                  pl.BlockSpec((d, tn), lambda i, j: (0, j)),
                  pl.BlockSpec((d, LANES), lambda i, j: (0, 0))],
        out_specs=[pl.BlockSpec((tm, tn), lambda i, j: (i, j)),
                   pl.BlockSpec((tm, LANES), lambda i, j: (i, 0))],
        out_shape=[SDS((m, n), F32), SDS((m, LANES), F32)],
        scratch_shapes=[pltpu.VMEM((tm, d), BF16)],
        compiler_params=_cparams(("parallel", "arbitrary"),
                                 [((tm, d), F32, 3), ((tm, d), BF16, 1), ((d, tn), BF16, 2),
                                  ((tm, tn), F32, 3), ((tm, LANES), F32, 2), ((d, LANES), BF16, 2)]),
        name="in_proj",
    )(x, g, w, wdt)


_NT = (((1,), (1,)), ((), ()))
LOG2E = 1.4426950408889634


def _sb_logs(z):
    neg_abs = lax.bitcast_convert_type(lax.bitcast_convert_type(z, jnp.uint32) | jnp.uint32(0x80000000), F32)
    lbeta = jnp.minimum(z, 0.0) - jnp.log2(1.0 + jnp.exp2(neg_abs))
    return lbeta, lbeta - z


def _sb_update(qm, kblk, vblk, bias, uo, carry, acc, mask):
    z = lax.dot_general(qm, kblk, _NT, preferred_element_type=F32) + bias
    lbeta, lk = _sb_logs(z)
    lk = jnp.where(mask, lk, 0.0)
    r = jnp.dot(lk.astype(BF16), uo, preferred_element_type=F32)
    w = jnp.where(mask, jnp.exp2(lbeta + r[:, :TK_ATT] + carry), 0.0)
    acc = acc + jnp.dot(w.astype(BF16), vblk, preferred_element_type=F32)
    return carry + r[:, TK_ATT:], acc


def _cumsum_matrix():
    jp = lax.broadcasted_iota(jnp.int32, (TK_ATT, 2 * TK_ATT), 0)
    j = lax.broadcasted_iota(jnp.int32, (TK_ATT, 2 * TK_ATT), 1)
    return jnp.where((jp > j) | (j >= TK_ATT), 1.0, 0.0).astype(BF16)


def _att_prompt_body(bias_ref, q_ref, k_ref, v_ref, g_ref, uo_ref, o_ref, carry_ref, acc_ref, *, tq, scale):
    hp = pl.program_id(1)
    qi = pl.program_id(2)
    nsub = tq // TK_ATT
    lane = lax.broadcasted_iota(jnp.int32, (1, LANES), 1)
    low = lane < D_HEAD_ATT
    uo = uo_ref[...]
    qb = (q_ref[...] * (scale * LOG2E)).astype(BF16)
    lane2 = lax.broadcasted_iota(jnp.int32, (1, 2 * TK_ATT), 1)
    bias = jnp.where(lane2 < TK_ATT, bias_ref[2 * hp], bias_ref[2 * hp + 1]) * LOG2E
    qrow = lax.broadcasted_iota(jnp.int32, (tq, 1), 0)
    carry_ref[...] = jnp.zeros_like(carry_ref)
    acc_ref[...] = jnp.zeros_like(acc_ref)

    def span_step(ks, nblk, masked):
        k0 = pl.multiple_of(ks * tq, tq)
        carries = [carry_ref[0], carry_ref[1]]
        accs = [acc_ref[0], acc_ref[1]]
        order = list(reversed(range(nblk)))
        zs, vblks, masks = [], [], []
        for sb in order:
            rows = pl.ds(k0 + sb * TK_ATT, TK_ATT)
            kf = k_ref[rows, :]
            kbd = jnp.concatenate([jnp.where(low, kf, 0.0), jnp.where(low, 0.0, kf)], axis=0).astype(BF16)
            vblks.append(v_ref[rows, :].astype(BF16))
            zs.append(lax.dot_general(qb, kbd, _NT, preferred_element_type=F32) + bias)
            masks.append((sb * TK_ATT + (lane2 & (TK_ATT - 1)) < qrow) if masked else None)
        lbetas, rs = [], []
        for z, mask in zip(zs, masks):
            lbeta, lk = _sb_logs(z)
            if masked:
                lk = jnp.where(mask, lk, 0.0)
            lkb = lk.astype(BF16)
            lbetas.append(lbeta)
            rs.append([jnp.dot(lkb[:, hh * TK_ATT:(hh + 1) * TK_ATT], uo, preferred_element_type=F32)
                       for hh in range(2)])
        for lbeta, r2, mask, vblk in zip(lbetas, rs, masks, vblks):
            for hh in range(2):
                hs = slice(hh * TK_ATT, (hh + 1) * TK_ATT)
                w = jnp.exp2(lbeta[:, hs] + r2[hh][:, :TK_ATT] + carries[hh])
                if masked:
                    w = jnp.where(mask[:, hs], w, 0.0)
                accs[hh] = accs[hh] + jnp.dot(w.astype(BF16), vblk, preferred_element_type=F32)
                carries[hh] = carries[hh] + r2[hh][:, TK_ATT:]
        for hh in range(2):
            carry_ref[hh] = carries[hh]
            acc_ref[hh] = accs[hh]

    span_step(qi, nsub, True)

    @pl.when((qi & 1) == 1)
    def _():
        span_step(qi - 1, nsub, False)

    nwide = lax.shift_right_logical(qi, 1)

    def off_diag(t, _):
        span_step(2 * (nwide - 1 - t), 2 * nsub, False)
        return 0

    lax.fori_loop(0, nwide, off_diag, 0)
    o = jnp.where(low, acc_ref[0], acc_ref[1])
    sq = o * o
    s0 = jnp.sum(jnp.where(low, sq, 0.0), axis=-1, keepdims=True)
    s1 = jnp.sum(jnp.where(low, 0.0, sq), axis=-1, keepdims=True)
    ms = jnp.where(low, s0, s1) * (1.0 / D_HEAD_ATT)
    o_ref[...] = ((o * lax.rsqrt(ms + EPS)) * g_ref[...]).astype(o_ref.dtype)


def _att_prompt(proj, bias, g, uo, nb, seq):
    m = proj.shape[0]
    tq = _tile(seq, 256)
    nq = seq // tq
    npair = H_ATT // 2
    body = functools.partial(_att_prompt_body, tq=tq, scale=D_HEAD_ATT ** -0.5)
    return pl.pallas_call(
        body,
        grid=(nb, npair, nq),
        in_specs=[pl.BlockSpec(memory_space=pltpu.SMEM),
                  pl.BlockSpec((tq, LANES), lambda b, h, i: (b * nq + i, COL_Q // LANES + h)),
                  pl.BlockSpec((seq, LANES), lambda b, h, i: (b, COL_K // LANES + h)),
                  pl.BlockSpec((seq, LANES), lambda b, h, i: (b, COL_V // LANES + h)),
                  pl.BlockSpec((1, LANES), lambda b, h, i: (0, h)),
                  pl.BlockSpec((TK_ATT, 2 * TK_ATT), lambda b, h, i: (0, 0))],
        out_specs=pl.BlockSpec((tq, LANES), lambda b, h, i: (b * nq + i, h)),
        out_shape=SDS((m, D_ATT), BF16),
        scratch_shapes=[pltpu.VMEM((2, tq, TK_ATT), F32), pltpu.VMEM((2, tq, LANES), F32)],
        compiler_params=_cparams(("parallel", "parallel", "arbitrary"),
                                 [((seq, LANES), F32, 4), ((tq, LANES), F32, 24)]),
        name="att_prompt",
    )(bias, proj, proj, proj, g, uo)


def _att_sample_body(pt_ref, q_ref, kn_ref, vn_ref, bias_ref, g_ref, uo_ref, *rest, npg, scale):
    k_refs, v_refs = rest[:npg], rest[npg:2 * npg]
    o_ref, qbd_ref, kpad_ref, vpad_ref, carry_ref, acc_ref = rest[2 * npg:]
    del pt_ref
    p = pl.program_id(1)
    nrow = H_ATT * SUBLANES
    row = lax.broadcasted_iota(jnp.int32, (nrow, D_ATT), 0)
    lane = lax.broadcasted_iota(jnp.int32, (nrow, D_ATT), 1)
    own = (row >> 3) == (lane >> 6)
    uo = uo_ref[...]
    bias = bias_ref[...] * LOG2E

    @pl.when(p == 0)
    def _():
        q = q_ref[...] * (scale * LOG2E)
        qt = jnp.concatenate([q] * H_ATT, axis=0)
        qbd = jnp.where(own, qt, 0.0).astype(BF16)
        qbd_ref[...] = qbd
        kpad_ref[...] = jnp.zeros_like(kpad_ref)
        vpad_ref[...] = jnp.zeros_like(vpad_ref)
        kpad_ref[0:SUBLANES, :] = kn_ref[...]
        vpad_ref[0:SUBLANES, :] = vn_ref[...]
        tok = lax.broadcasted_iota(jnp.int32, (nrow, 1), 0) & (SUBLANES - 1)
        mask = lax.broadcasted_iota(jnp.int32, (1, TK_ATT), 1) < tok
        carry, acc = _sb_update(qbd, kpad_ref[...].astype(BF16), vpad_ref[...].astype(BF16), bias, uo,
                                jnp.zeros((nrow, TK_ATT), F32), jnp.zeros((nrow, D_ATT), F32), mask)
        carry_ref[...] = carry
        acc_ref[...] = acc

    carry, acc = carry_ref[...], acc_ref[...]
    qbd = qbd_ref[...]
    zs = [jnp.dot(qbd, k_refs[gi][0].astype(BF16), preferred_element_type=F32) + bias for gi in range(npg)]
    lbetas, rs = [], []
    for z in zs:
        lbeta, lk = _sb_logs(z)
        lbetas.append(lbeta)
        rs.append(jnp.dot(lk.astype(BF16), uo, preferred_element_type=F32))
    for gi in range(npg):
        w = jnp.exp2(lbetas[gi] + rs[gi][:, :TK_ATT] + carry)
        acc = acc + lax.dot_general(w.astype(BF16), v_refs[gi][0].astype(BF16), _NT, preferred_element_type=F32)
        carry = carry + rs[gi][:, TK_ATT:]
    carry_ref[...] = carry
    acc_ref[...] = acc

    @pl.when(p == pl.num_programs(1) - 1)
    def _():
        a = jnp.where(own, acc, 0.0)
        ms = jnp.sum(a * a, axis=-1, keepdims=True) * (1.0 / D_HEAD_ATT)
        a = a * lax.rsqrt(ms + EPS)
        o = a[0:SUBLANES]
        for h in range(1, H_ATT):
            o = o + a[h * SUBLANES:(h + 1) * SUBLANES]
        o_ref[...] = o * g_ref[...]


def _att_sample(proj, cache_k, cache_v, page_table, layer, n_pool, bias_rows, g, uo):
    m = proj.shape[0]
    nseq, npages = page_table.shape
    assert m == nseq * SUBLANES
    npg = next(n for n in (8, 4, 2, 1) if npages % n == 0)
    pt = page_table.reshape(-1)
    ck = jnp.transpose(cache_k, (0, 1, 3, 4, 2)).reshape(-1, D_ATT, PAGE_SIZE)
    cv = jnp.transpose(cache_v, (0, 1, 3, 4, 2)).reshape(-1, D_ATT, PAGE_SIZE)
    nrow = H_ATT * SUBLANES

    def page_map(gi):
        def f(b, p, pt_ref):
            return (layer * n_pool + pt_ref[b * npages + (npages - 1 - (p * npg + gi))], 0, 0)
        return f

    page_specs = [pl.BlockSpec((1, D_ATT, PAGE_SIZE), page_map(gi)) for gi in range(npg)]
    body = functools.partial(_att_sample_body, npg=npg, scale=D_HEAD_ATT ** -0.5)
    grid_spec = pltpu.PrefetchScalarGridSpec(
        num_scalar_prefetch=1,
        grid=(nseq, npages // npg),
        in_specs=[pl.BlockSpec((SUBLANES, D_ATT), lambda b, p, pt_ref: (b, COL_Q // D_ATT)),
                  pl.BlockSpec((SUBLANES, D_ATT), lambda b, p, pt_ref: (b, COL_K // D_ATT)),
                  pl.BlockSpec((SUBLANES, D_ATT), lambda b, p, pt_ref: (b, COL_V // D_ATT)),
                  pl.BlockSpec((nrow, TK_ATT), lambda b, p, pt_ref: (0, 0)),
                  pl.BlockSpec((1, D_ATT), lambda b, p, pt_ref: (0, 0)),
                  pl.BlockSpec((TK_ATT, 2 * TK_ATT), lambda b, p, pt_ref: (0, 0))] + page_specs + page_specs,
        out_specs=pl.BlockSpec((SUBLANES, D_ATT), lambda b, p, pt_ref: (b, 0)),
        scratch_shapes=[pltpu.VMEM((nrow, D_ATT), BF16),
                        pltpu.VMEM((TK_ATT, D_ATT), F32),
                        pltpu.VMEM((TK_ATT, D_ATT), F32),
                        pltpu.VMEM((nrow, TK_ATT), F32),
                        pltpu.VMEM((nrow, D_ATT), F32)])
    return pl.pallas_call(
        body,
        grid_spec=grid_spec,
        out_shape=SDS((m, D_ATT), F32),
        compiler_params=_cparams(("parallel", "arbitrary"),
                                 [((PAGE_SIZE, D_ATT), F32, 4 * npg), ((TK_ATT, D_ATT), F32, 4)]),
        name="att_sample",
    )(pt, proj, proj, proj, bias_rows, g, uo, *([ck] * npg), *([cv] * npg))


def _ssd_body(*refs, lr, nc, has_h0):
    if has_h0:
        (z_ref, xs_ref, bc_ref, dt_ref, c0_ref, h0_ref, cw_ref, cb_ref, dtb_ref, alog_ref, dd_ref, ng_ref,
         tri_ref, y_ref, ht_ref, win_ref, hs_ref, dts_ref, ysc_ref, xd_ref, et_ref) = refs
    else:
        (z_ref, xs_ref, bc_ref, dt_ref, c0_ref, cw_ref, cb_ref, dtb_ref, alog_ref, dd_ref, ng_ref,
         tri_ref, y_ref, ht_ref, win_ref, hs_ref, dts_ref, ysc_ref, xd_ref, et_ref) = refs
        h0_ref = None
    L = SSD_CHUNK
    c = pl.program_id(1)
    hist = SUBLANES

    @pl.when(c == 0)
    def _():
        win_ref[0:hist, :] = c0_ref[0]
        if lr < L:
            win_ref[hist:hist + L, :] = jnp.zeros((L, SSD_CONV_DIM), F32)
            dts_ref[...] = jnp.zeros_like(dts_ref)
        if has_h0:
            hs_ref[...] = h0_ref[0]
        else:
            hs_ref[...] = jnp.zeros_like(hs_ref)

    win_ref[hist:hist + lr, 0:D_SSD] = xs_ref[...]
    win_ref[hist:hist + lr, D_SSD:SSD_CONV_DIM] = bc_ref[...]
    dts_ref[0:lr, :] = dt_ref[...]

    pre = cb_ref[...]
    for k in range(SSD_CONV_W):
        off = hist - (SSD_CONV_W - 1) + k
        pre = pre + cw_ref[k:k + 1, :] * win_ref[off:off + L, :]
    xbc = pre * _sigmoid(pre)
    win_ref[0:hist, :] = win_ref[L:L + hist, :]

    lane = lax.broadcasted_iota(jnp.int32, (1, LANES), 1)
    low = lane < P_SSD
    rowi = lax.broadcasted_iota(jnp.int32, (L, 1), 0)
    dt = _softplus(dts_ref[...] + dtb_ref[...])
    if lr < L:
        dt = jnp.where(rowi < lr, dt, 0.0)
    a_head = jnp.where(lane < H_SSD, -jnp.exp(alog_ref[...]), 0.0)
    a = dt * a_head
    tri = tri_ref[...]
    a1, a2, a3 = _split3(a)
    a_cs = (jnp.dot(tri, a1, preferred_element_type=F32) + jnp.dot(tri, a2, preferred_element_type=F32)
            + jnp.dot(tri, a3, preferred_element_type=F32))
    a_cs_t = a_cs.T
    a_tot = a_cs[L - 1:L, :]
    causal = rowi >= lax.broadcasted_iota(jnp.int32, (1, L), 1)

    cbs, bts, cgs = [], [], []
    for g in range(G_SSD):
        bg = xbc[:, D_SSD + g * N_SSD:D_SSD + (g + 1) * N_SSD]
        cg = xbc[:, D_SSD + (G_SSD + g) * N_SSD:D_SSD + (G_SSD + g + 1) * N_SSD].astype(BF16)
        cbs.append(lax.dot_general(cg, bg.astype(BF16), (((1,), (1,)), ((), ())), preferred_element_type=F32))
        bts.append(bg.T.astype(BF16))
        cgs.append(cg)

    pairs_per_group = H_SSD // G_SSD // 2
    for hp in range(H_SSD // 2):
        g = hp // pairs_per_group
        cols, dtcols, mats = [], [], []
        for h in (2 * hp, 2 * hp + 1):
            col = jnp.broadcast_to(a_cs[:, h:h + 1], (L, LANES))
            dec = jnp.exp(jnp.where(causal, col - a_cs_t[h:h + 1, :], -1e30))
            mats.append((cbs[g] * dec).astype(BF16))
            cols.append(col)
            dtcols.append(jnp.broadcast_to(dt[:, h:h + 1], (L, LANES)))
        sl = slice(hp * LANES, (hp + 1) * LANES)
        xs_pair = xbc[:, sl]
        xdt = xs_pair * jnp.where(low, dtcols[0], dtcols[1])
        xdt_bf = xdt.astype(BF16)
        ydiag = jnp.where(low, jnp.dot(mats[0], xdt_bf, preferred_element_type=F32),
                          jnp.dot(mats[1], xdt_bf, preferred_element_type=F32))
        cs_pair = jnp.where(low, cols[0], cols[1])
        gsl = slice((hp % pairs_per_group) * LANES, (hp % pairs_per_group + 1) * LANES)
        yoff = jnp.dot(cgs[g], hs_ref[g, :, gsl].astype(BF16), preferred_element_type=F32) * jnp.exp(cs_pair)
        ysc_ref[:, sl] = ydiag + yoff + dd_ref[:, sl] * xs_pair
        tot_pair = jnp.where(low, jnp.broadcast_to(a_tot[:, 2 * hp:2 * hp + 1], (1, LANES)),
                             jnp.broadcast_to(a_tot[:, 2 * hp + 1:2 * hp + 2], (1, LANES)))
        xd_ref[:, sl] = (xdt * jnp.exp(tot_pair - cs_pair)).astype(BF16)
        et_ref[:, sl] = jnp.exp(tot_pair)

    gw = D_SSD // G_SSD
    for g in range(G_SSD):
        gs = slice(g * gw, (g + 1) * gw)
        cst = jnp.dot(bts[g], xd_ref[:, gs], preferred_element_type=F32)
        hs_ref[g] = hs_ref[g] * et_ref[:, gs] + cst

    zz = z_ref[...]
    y = ysc_ref[0:lr, :] * (zz * _sigmoid(zz))
    for g in range(G_SSD):
        gs = slice(g * gw, (g + 1) * gw)
        yg = y[:, gs]
        ms = jnp.mean(yg * yg, axis=-1, keepdims=True)
        y_ref[:, gs] = ((yg * lax.rsqrt(ms + EPS)) * ng_ref[:, gs]).astype(y_ref.dtype)

    @pl.when(c == nc - 1)
    def _():
        ht_ref[0] = hs_ref[...]


def _ssd(proj, dtp, conv0, h0, cw, cb, dtb, alog, dd, ng, tri, nb, seq, out_dtype):
    m = proj.shape[0]
    lr = min(seq, SSD_CHUNK)
    assert seq % lr == 0
    nc = seq // lr
    has_h0 = h0 is not None
    gw = D_SSD // G_SSD
    row = lambda b, c: b * nc + c
    in_specs = [pl.BlockSpec((lr, D_SSD), lambda b, c: (row(b, c), COL_Z // D_SSD)),
                pl.BlockSpec((lr, D_SSD), lambda b, c: (row(b, c), COL_XS // D_SSD)),
                pl.BlockSpec((lr, 2 * G_SSD * N_SSD), lambda b, c: (row(b, c), COL_BC // (2 * G_SSD * N_SSD))),
                pl.BlockSpec((lr, LANES), lambda b, c: (row(b, c), 0)),
                pl.BlockSpec((1, SUBLANES, SSD_CONV_DIM), lambda b, c: (b, 0, 0))]
    args = [proj, proj, proj, dtp, conv0]
    if has_h0:
        in_specs.append(pl.BlockSpec((1, G_SSD, N_SSD, gw), lambda b, c: (b, 0, 0, 0)))
        args.append(h0)
    const = lambda shape: pl.BlockSpec(shape, lambda b, c: (0,) * len(shape))
    in_specs += [const((SSD_CONV_W, SSD_CONV_DIM)), const((1, SSD_CONV_DIM)), const((1, LANES)),
                 const((1, LANES)), const((1, D_SSD)), const((1, D_SSD)), const((SSD_CHUNK, SSD_CHUNK))]
    args += [cw, cb, dtb, alog, dd, ng, tri]
    body = functools.partial(_ssd_body, lr=lr, nc=nc, has_h0=has_h0)
    L = SSD_CHUNK
    return pl.pallas_call(
        body,
        grid=(nb, nc),
        in_specs=in_specs,
        out_specs=[pl.BlockSpec((lr, D_SSD), lambda b, c: (row(b, c), 0)),
                   pl.BlockSpec((1, G_SSD, N_SSD, gw), lambda b, c: (b, 0, 0, 0))],
        out_shape=[SDS((m, D_SSD), out_dtype), SDS((nb, G_SSD, N_SSD, gw), F32)],
        scratch_shapes=[pltpu.VMEM((L + SUBLANES, SSD_CONV_DIM), F32),
                        pltpu.VMEM((G_SSD, N_SSD, gw), F32),
                        pltpu.VMEM((L, LANES), F32),
                        pltpu.VMEM((L, D_SSD), F32),
                        pltpu.VMEM((L, D_SSD), BF16),
                        pltpu.VMEM((1, D_SSD), F32)],
        compiler_params=_cparams(("parallel", "arbitrary"),
                                 [((L, SSD_CONV_DIM), F32, 12), ((G_SSD, N_SSD, gw), F32, 5)]),
        name="ssd",
    )(*args)


def _conf_body(glu_ref, c0_ref, w_ref, b_ref, lg_ref, lb_ref, o_ref, tail_ref, win_ref, *, tt, nt):
    t = pl.program_id(1)

    @pl.when(t == 0)
    def _():
        win_ref[0:CONF_PAD, :] = c0_ref[0]

    glu = glu_ref[...]
    win_ref[CONF_PAD:CONF_PAD + tt, :] = glu[:, :C_CONF] * _sigmoid(glu[:, C_CONF:])
    acc = jnp.broadcast_to(b_ref[...], (tt, C_CONF))
    for k in range(CONF_W):
        off = CONF_PAD - (CONF_W - 1) + k
        acc = acc + w_ref[k:k + 1, :] * win_ref[off:off + tt, :]
    mu = jnp.mean(acc, axis=-1, keepdims=True)
    xc = acc - mu
    var = jnp.mean(xc * xc, axis=-1, keepdims=True)
    yn = (xc * lax.rsqrt(var + EPS)) * lg_ref[...] + lb_ref[...]
    o_ref[...] = (yn * _sigmoid(yn)).astype(o_ref.dtype)
    tail = win_ref[tt:tt + CONF_PAD, :]
    tail_ref[0] = tail
    if nt > 1:
        win_ref[0:CONF_PAD, :] = tail


def _conformer(proj, conv0, w, b, lg, lb, nb, seq, out_dtype):
    m = proj.shape[0]
    tt = _tile(seq, 256)
    nt = seq // tt
    assert nt == 1 or tt >= CONF_PAD
    body = functools.partial(_conf_body, tt=tt, nt=nt)
    const = lambda shape: pl.BlockSpec(shape, lambda bb, t: (0,) * len(shape))
    return pl.pallas_call(
        body,
        grid=(nb, nt),
        in_specs=[pl.BlockSpec((tt, 2 * C_CONF), lambda bb, t: (bb * nt + t, COL_GLU // (2 * C_CONF))),
                  pl.BlockSpec((1, CONF_PAD, C_CONF), lambda bb, t: (bb, 0, 0)),
                  const((CONF_W, C_CONF)), const((1, C_CONF)), const((1, C_CONF)), const((1, C_CONF))],
        out_specs=[pl.BlockSpec((tt, C_CONF), lambda bb, t: (bb * nt + t, 0)),
                   pl.BlockSpec((1, CONF_PAD, C_CONF), lambda bb, t: (bb, 0, 0))],
        out_shape=[SDS((m, C_CONF), out_dtype), SDS((nb, CONF_PAD, C_CONF), F32)],
        scratch_shapes=[pltpu.VMEM((CONF_PAD + tt, C_CONF), F32)],
        compiler_params=_cparams(("parallel", "arbitrary"), [((tt, 2 * C_CONF), F32, 8)]),
        name="conformer",
    )(proj, conv0, w, b, lg, lb)


def _outproj_body(x_ref, a_ref, y_ref, u_ref, w_ref, g_ref, x1_ref, h_ref):
    acc = x_ref[...]
    acc = acc + jnp.dot(a_ref[...].astype(BF16), w_ref[0:D_ATT, :], preferred_element_type=F32)
    acc = acc + jnp.dot(y_ref[...].astype(BF16), w_ref[D_ATT:D_ATT + D_SSD, :], preferred_element_type=F32)
    acc = acc + jnp.dot(u_ref[...].astype(BF16), w_ref[D_ATT + D_SSD:, :], preferred_element_type=F32)
    x1_ref[...] = acc
    ms = jnp.mean(acc * acc, axis=-1, keepdims=True)
    h_ref[...] = ((acc * lax.rsqrt(ms + EPS)) * g_ref[...]).astype(BF16)


def _out_proj(x, att, y, u, w, g):
    m, d = x.shape
    tm = _tile(m, 512)
    row = lambda width: pl.BlockSpec((tm, width), lambda i: (i, 0))
    return pl.pallas_call(
        _outproj_body,
        grid=(m // tm,),
        in_specs=[row(d), row(D_ATT), row(D_SSD), row(C_CONF),
                  pl.BlockSpec((d, d), lambda i: (0, 0)), pl.BlockSpec((1, d), lambda i: (0, 0))],
        out_specs=[row(d), row(d)],
        out_shape=[SDS((m, d), F32), SDS((m, d), BF16)],
        compiler_params=_cparams(("parallel",), [((tm, d), F32, 7), ((d, d), BF16, 2)]),
        name="out_proj",
    )(x, att, y, u, w, g)


HALO = 16


def _ffn_up_body(*refs, tm, seq, sample):
    if sample:
        h_ref, wg_ref, wv_ref, cw_ref, cb_ref, s1_ref, s2_ref, act_ref, tail_ref, lhs_ref, gs_ref = refs
    else:
        h_ref, halo_ref, wg_ref, wv_ref, cw_ref, cb_ref, act_ref, tail_ref, lhs_ref, gs_ref = refs
    i = pl.program_id(0)

    @pl.when(pl.program_id(1) == 0)
    def _():
        lhs_ref[HALO:, :] = h_ref[...]
        if sample:
            lhs_ref[0:HALO, :] = jnp.zeros((HALO, D_MODEL), BF16)
        else:
            lhs_ref[0:HALO, :] = halo_ref[...]

            @pl.when((i * tm) % seq == 0)
            def _():
                lhs_ref[0:HALO, :] = jnp.zeros((HALO, D_MODEL), BF16)

    gs_ref[...] = jnp.dot(lhs_ref[...], wg_ref[...], preferred_element_type=F32)
    val = jnp.dot(lhs_ref[HALO:, :], wv_ref[...], preferred_element_type=F32)
    g0 = gs_ref[HALO:HALO + tm, :]
    g1 = gs_ref[HALO - 1:HALO - 1 + tm, :]
    g2 = gs_ref[HALO - 2:HALO - 2 + tm, :]
    if sample:
        tok = lax.broadcasted_iota(jnp.int32, (tm, 1), 0) & (seq - 1)
        g1 = jnp.where(tok < 1, s1_ref[...], g1)
        g2 = jnp.where(tok < 2, s2_ref[...], g2)
        tail_ref[...] = g0
    else:
        tail_ref[...] = gs_ref[HALO + tm - SUBLANES:HALO + tm, :]
    pre = cw_ref[0:1, :] * g2 + cw_ref[1:2, :] * g1 + cw_ref[2:3, :] * g0 + cb_ref[...]
    act_ref[...] = ((pre * _sigmoid(pre)) * val).astype(BF16)


def _ffn_up(h, wg, wv, cw, cb, seq, s1=None, s2=None):
    m, d = h.shape
    f = wg.shape[1]
    sample = s1 is not None
    tm, tn = _tile(m, m if sample else min(1024, seq)), _tile(f, 512)
    if sample:
        assert seq & (seq - 1) == 0 and seq >= FFN_CONV_W - 1
    else:
        assert seq % tm == 0
    body = functools.partial(_ffn_up_body, tm=tm, seq=seq, sample=sample)
    hspec = pl.BlockSpec((tm, d), lambda i, j: (i, 0))
    wspec = pl.BlockSpec((d, tn), lambda i, j: (0, j))
    cspec = lambda r: pl.BlockSpec((r, tn), lambda i, j: (0, j))
    tile = pl.BlockSpec((tm, tn), lambda i, j: (i, j))
    if sample:
        in_specs = [hspec, wspec, wspec, cspec(FFN_CONV_W), cspec(1), tile, tile]
        args = (h, wg, wv, cw, cb, s1, s2)
        tail_rows, tail_spec = m, tile
    else:
        halo = pl.BlockSpec((HALO, d), lambda i, j: (jnp.maximum(i * (tm // HALO) - 1, 0), 0))
        in_specs = [hspec, halo, wspec, wspec, cspec(FFN_CONV_W), cspec(1)]
        args = (h, h, wg, wv, cw, cb)
        tail_rows, tail_spec = (m // tm) * SUBLANES, pl.BlockSpec((SUBLANES, tn), lambda i, j: (i, j))
    return pl.pallas_call(
        body,
        grid=(m // tm, f // tn),
        in_specs=in_specs,
        out_specs=[tile, tail_spec],
        out_shape=[SDS((m, f), BF16), SDS((tail_rows, f), F32)],
        scratch_shapes=[pltpu.VMEM((HALO + tm, d), BF16), pltpu.VMEM((HALO + tm, tn), F32)],
        compiler_params=_cparams(("parallel", "arbitrary"),
                                 [((tm, d), BF16, 3), ((d, tn), BF16, 4), ((tm, tn), F32, 8)]),
        name="ffn_up",
    )(*args)


def _ffn_down_body(x_ref, a_ref, w_ref, o_ref):
    o_ref[...] = x_ref[...] + jnp.dot(a_ref[...], w_ref[...], preferred_element_type=F32)


def _ffn_down(x, act, w):
    m, d = x.shape
    f = act.shape[1]
    tm, tn = _tile(m, 512), _tile(d, 512)
    return pl.pallas_call(
        _ffn_down_body,
        grid=(m // tm, d // tn),
        in_specs=[pl.BlockSpec((tm, tn), lambda i, j: (i, j)),
                  pl.BlockSpec((tm, f), lambda i, j: (i, 0)),
                  pl.BlockSpec((f, tn), lambda i, j: (0, j))],
        out_specs=pl.BlockSpec((tm, tn), lambda i, j: (i, j)),
        out_shape=SDS((m, d), F32),
        compiler_params=_cparams(("parallel", "arbitrary"),
                                 [((tm, f), BF16, 2), ((f, tn), BF16, 2), ((tm, tn), F32, 6)]),
        name="ffn_down",
    )(x, act, w)


def _final_norm_body(x_ref, g_ref, o_ref):
    x = x_ref[...]
    ms = jnp.mean(x * x, axis=-1, keepdims=True)
    o_ref[...] = (x * lax.rsqrt(ms + EPS)) * g_ref[...]


def _final_norm(x, g):
    m, d = x.shape
    tm = _tile(m, 512)
    return pl.pallas_call(
        _final_norm_body,
        grid=(m // tm,),
        in_specs=[pl.BlockSpec((tm, d), lambda i: (i, 0)), pl.BlockSpec((1, d), lambda i: (0, 0))],
        out_specs=pl.BlockSpec((tm, d), lambda i: (i, 0)),
        out_shape=SDS((m, d), F32),
        compiler_params=_cparams(("parallel",), [((tm, d), F32, 6)]),
        name="final_norm",
    )(x, g)


def _pad_cols(x, width):
    return jnp.pad(x, ((0, 0), (0, width - x.shape[1])))


def _layer_weights(l, p):
    w_in = p["w_in"][l]
    c_z, c_x, c_dt = 3 * D_ATT, 3 * D_ATT + D_SSD, 3 * D_ATT + D_SSD + SSD_CONV_DIM
    c_glu = c_dt + H_SSD
    order = [w_in[:, c_z:c_z + D_SSD],
             w_in[:, c_x:c_x + D_SSD],
             w_in[:, c_glu:c_glu + 2 * C_CONF],
             w_in[:, c_x + D_SSD:c_dt],
             w_in[:, 0:3 * D_ATT]]
    w_up = p["w_up"][l]
    row = lambda v: v.reshape(1, -1)
    return dict(
        w_main=jnp.concatenate(order, axis=1).astype(BF16),
        w_dt=_pad_cols(w_in[:, c_dt:c_dt + H_SSD], LANES).astype(BF16),
        g_mix=row(p["norm_mix_g"][l]),
        att_bias=p["att_logit_bias"][l],
        att_g=row(p["att_norm_g"][l]),
        ssd_cw=p["ssd_conv_w"][l], ssd_cb=row(p["ssd_conv_b"][l]),
        dtb=_pad_cols(row(p["ssd_dt_bias"][l]), LANES), alog=_pad_cols(row(p["ssd_a_log"][l]), LANES),
        dd=row(jnp.repeat(p["ssd_d"][l], P_SSD)), ssd_ng=row(p["ssd_norm_g"][l]),
        conf_w=p["conf_conv_w"][l], conf_b=row(p["conf_conv_b"][l]),
        conf_lg=row(p["conf_ln_g"][l]), conf_lb=row(p["conf_ln_b"][l]),
        w_out=p["w_out"][l].astype(BF16), g_ffn=row(p["norm_ffn_g"][l]),
        w_gate=_pad_cols(w_up[:, :D_FF], D_FF_PAD).astype(BF16),
        w_val=_pad_cols(w_up[:, D_FF:], D_FF_PAD).astype(BF16),
        ffn_cw=_pad_cols(p["ffn_conv_w"][l], D_FF_PAD), ffn_cb=_pad_cols(row(p["ffn_conv_b"][l]), D_FF_PAD),
        w_down=jnp.pad(p["w_down"][l], ((0, D_FF_PAD - D_FF), (0, 0))).astype(BF16),
    )


def _ssm_to_kernel_layout(h):
    b = h.shape[0]
    e = H_SSD // G_SSD
    return h.reshape(b, G_SSD, e, P_SSD, N_SSD).transpose(0, 1, 4, 2, 3).reshape(b, G_SSD, N_SSD, e * P_SSD)


def _ssm_from_kernel_layout(h):
    b = h.shape[0]
    e = H_SSD // G_SSD
    return h.reshape(b, G_SSD, N_SSD, e, P_SSD).transpose(0, 1, 3, 4, 2).reshape(b, H_SSD, P_SSD, N_SSD)


def _xbc_cols(proj3):
    return jnp.concatenate([proj3[..., COL_XS:COL_XS + D_SSD], proj3[..., COL_BC:COL_BC + 2 * G_SSD * N_SSD]],
                           axis=-1)


def kernel(x_prompt, x_sample, cache_k, cache_v, state_ssm, state_ssd_conv, state_conf_conv, state_ffn_conv, page_table, norm_mix_g, w_in, att_logit_bias, att_norm_g, ssd_conv_w, ssd_conv_b, ssd_dt_bias, ssd_a_log, ssd_d, ssd_norm_g, conf_conv_w, conf_conv_b, conf_ln_g, conf_ln_b, w_out, norm_ffn_g, w_up, ffn_conv_w, ffn_conv_b, w_down, norm_final_g):
    params = dict(norm_mix_g=norm_mix_g, w_in=w_in, att_logit_bias=att_logit_bias, att_norm_g=att_norm_g,
                  ssd_conv_w=ssd_conv_w, ssd_conv_b=ssd_conv_b, ssd_dt_bias=ssd_dt_bias, ssd_a_log=ssd_a_log,
                  ssd_d=ssd_d, ssd_norm_g=ssd_norm_g, conf_conv_w=conf_conv_w, conf_conv_b=conf_conv_b,
                  conf_ln_g=conf_ln_g, conf_ln_b=conf_ln_b, w_out=w_out, norm_ffn_g=norm_ffn_g, w_up=w_up,
                  ffn_conv_w=ffn_conv_w, ffn_conv_b=ffn_conv_b, w_down=w_down)
    depth = w_in.shape[0]
    bp, sp, d = x_prompt.shape
    bs, ss, _ = x_sample.shape
    assert ss == SUBLANES and d == D_MODEL
    n_pool = cache_k.shape[1]
    xp = x_prompt.reshape(bp * sp, d)
    xs = x_sample.reshape(bs * ss, d)
    uo = _cumsum_matrix()
    tri = jnp.tril(jnp.ones((SSD_CHUNK, SSD_CHUNK), F32)).astype(BF16)
    zeros_ssd_conv = jnp.zeros((bp, SUBLANES, SSD_CONV_DIM), F32)
    zeros_conf_conv = jnp.zeros((bp, CONF_PAD, C_CONF), F32)
    outs_p, outs_s = [], []
    for l in range(depth):
        w = _layer_weights(l, params)

        proj, dtp = _in_proj(xp, w["g_mix"], w["w_main"], w["w_dt"])
        att = _att_prompt(proj, w["att_bias"], w["att_g"], uo, bp, sp)
        y, h_t = _ssd(proj, dtp, zeros_ssd_conv, None, w["ssd_cw"], w["ssd_cb"], w["dtb"], w["alog"], w["dd"],
                      w["ssd_ng"], tri, bp, sp, BF16)
        u, conf_tail = _conformer(proj, zeros_conf_conv, w["conf_w"], w["conf_b"], w["conf_lg"], w["conf_lb"],
                                  bp, sp, BF16)
        x1, h2 = _out_proj(xp, att, y, u, w["w_out"], w["g_ffn"])
        act, gate_tail = _ffn_up(h2, w["w_gate"], w["w_val"], w["ffn_cw"], w["ffn_cb"], sp)
        xp = _ffn_down(x1, act, w["w_down"])
        proj3 = proj.reshape(bp, sp, D_PROJ)
        tiles_per_seq = gate_tail.shape[0] // SUBLANES // bp
        gt = gate_tail.reshape(bp, tiles_per_seq, SUBLANES, D_FF_PAD)
        outs_p.append((
            proj3[..., COL_K:COL_K + D_ATT].reshape(bp, sp, H_ATT, D_HEAD_ATT),
            proj3[..., COL_V:COL_V + D_ATT].reshape(bp, sp, H_ATT, D_HEAD_ATT),
            _ssm_from_kernel_layout(h_t),
            _xbc_cols(proj3[:, sp - (SSD_CONV_W - 1):]),
            conf_tail[:, CONF_PAD - (CONF_W - 1):],
            gt[:, -1, SUBLANES - (FFN_CONV_W - 1):, :D_FF]))

        proj, dtp = _in_proj(xs, w["g_mix"], w["w_main"], w["w_dt"])
        bias_rows = jnp.broadcast_to(jnp.repeat(w["att_bias"], SUBLANES)[:, None], (H_ATT * SUBLANES, TK_ATT))
        att = _att_sample(proj, cache_k, cache_v, page_table, l, n_pool, bias_rows, w["att_g"], uo)
        conv0 = jnp.pad(state_ssd_conv[l], ((0, 0), (SUBLANES - (SSD_CONV_W - 1), 0), (0, 0)))
        y, h_t = _ssd(proj, dtp, conv0, _ssm_to_kernel_layout(state_ssm[l]), w["ssd_cw"], w["ssd_cb"], w["dtb"],
                      w["alog"], w["dd"], w["ssd_ng"], tri, bs, ss, F32)
        conf0 = jnp.pad(state_conf_conv[l], ((0, 0), (CONF_PAD - (CONF_W - 1), 0), (0, 0)))
        u, conf_tail = _conformer(proj, conf0, w["conf_w"], w["conf_b"], w["conf_lg"], w["conf_lb"], bs, ss, F32)
        x1, h2 = _out_proj(xs, att, y, u, w["w_out"], w["g_ffn"])
        st = _pad_cols(state_ffn_conv[l].reshape(bs * (FFN_CONV_W - 1), D_FF), D_FF_PAD).reshape(
            bs, FFN_CONV_W - 1, D_FF_PAD)
        s1 =jnp.concatenate([st[:, 1:2], jnp.zeros((bs, ss - 1, D_FF_PAD), F32)], axis=1).reshape(bs * ss, -1)
        s2 = jnp.concatenate([st, jnp.zeros((bs, ss - 2, D_FF_PAD), F32)], axis=1).reshape(bs * ss, -1)
        act, gate_full = _ffn_up(h2, w["w_gate"], w["w_val"], w["ffn_cw"], w["ffn_cb"], ss, s1, s2)
        xs = _ffn_down(x1, act, w["w_down"])
        proj3 = proj.reshape(bs, ss, D_PROJ)
        outs_s.append((
            proj3[..., COL_K:COL_K + D_ATT].reshape(bs, ss, H_ATT, D_HEAD_ATT),
            proj3[..., COL_V:COL_V + D_ATT].reshape(bs, ss, H_ATT, D_HEAD_ATT),
            _ssm_from_kernel_layout(h_t),
            _xbc_cols(proj3[:, ss - (SSD_CONV_W - 1):]),
            conf_tail[:, CONF_PAD - (CONF_W - 1):],
            gate_full.reshape(bs, ss, D_FF_PAD)[:, ss - (FFN_CONV_W - 1):, :D_FF]))

    g_final = norm_final_g.reshape(1, d)
    y_prompt = _final_norm(xp, g_final).reshape(bp, sp, d)
    y_sample = _final_norm(xs, g_final).reshape(bs, ss, d)
    stack = lambda outs, i: jnp.stack([o[i] for o in outs])
    return (y_prompt, y_sample,
            stack(outs_p, 0), stack(outs_p, 1), stack(outs_s, 0), stack(outs_s, 1),
            stack(outs_p, 2), stack(outs_s, 2), stack(outs_p, 3), stack(outs_s, 3),
            stack(outs_p, 4), stack(outs_s, 4), stack(outs_p, 5), stack(outs_s, 5))
```

```python
import functools
import math

import jax
import jax.numpy as jnp
from jax import lax
from jax.experimental import pallas as pl
from jax.experimental.pallas import tpu as pltpu

F32 = jnp.float32
BF16 = jnp.bfloat16
SDS = jax.ShapeDtypeStruct

D_MODEL = 2048
D_HEAD_ATT = 64
D_ATT = 512
H_ATT = 8
D_SSD = 1024
P_SSD = 64
H_SSD = 16
G_SSD = 2
N_SSD = 128
SSD_CONV_W = 4
SSD_CONV_DIM = D_SSD + 2 * G_SSD * N_SSD
SSD_CHUNK = 128
C_CONF = 512
CONF_W = 31
D_FF = 5504
FFN_CONV_W = 3
PAGE_SIZE = 128
EPS = 1e-6

LANES = 128
SUBLANES = 8
VMEM_BYTES_V7X = 64 * 1024 * 1024
VMEM_LIMIT_MAX = VMEM_BYTES_V7X - 8 * 1024 * 1024
VMEM_COMPILER_SCRATCH = 16 * 1024 * 1024

D_PROJ = 5120
COL_Z, COL_XS, COL_GLU, COL_BC, COL_Q, COL_K, COL_V = 0, 1024, 2048, 3072, 3584, 4096, 4608
FFN_TN = 512
CONF_PAD = 32
TK_ATT = 128


def _nbytes(shape, dtype):
    return math.prod(shape) * jnp.dtype(dtype).itemsize


def _cparams(sem, bufs):
    est = sum(_nbytes(s, d) * n for s, d, n in bufs)
    return pltpu.CompilerParams(dimension_semantics=sem,
                                vmem_limit_bytes=int(min(est + VMEM_COMPILER_SCRATCH, VMEM_LIMIT_MAX)))


def _tile(m, pref):
    t = min(m, pref)
    assert m % t == 0, (m, pref)
    return t


def _sigmoid(x):
    return 1.0 / (1.0 + jnp.exp(-x))


def _softplus(x):
    return jnp.maximum(x, 0.0) + jnp.log1p(jnp.exp(-jnp.abs(x)))


def _split3(x):
    a = x.astype(BF16)
    r = x - a.astype(F32)
    b = r.astype(BF16)
    c = (r - b.astype(F32)).astype(BF16)
    return a, b, c


def _inproj_body(x_ref, g_ref, w_ref, wdt_ref, o_ref, dt_ref, hn_ref):
    @pl.when(pl.program_id(1) == 0)
    def _():
        x = x_ref[...]
        ms = jnp.mean(x * x, axis=-1, keepdims=True)
        hn_ref[...] = ((x * lax.rsqrt(ms + EPS)) * g_ref[...]).astype(BF16)
        dt_ref[...] = jnp.dot(hn_ref[...], wdt_ref[...], preferred_element_type=F32)

    o_ref[...] = jnp.dot(hn_ref[...], w_ref[...], preferred_element_type=F32)


def _in_proj(x, g, w, wdt):
    m, d = x.shape
    n = w.shape[1]
    tm, tn = _tile(m, 1024), _tile(n, 1024)
    return pl.pallas_call(
        _inproj_body,
        grid=(m // tm, n // tn),
        in_specs=[pl.BlockSpec((tm, d), lambda i, j: (i, 0)),
                  pl.BlockSpec((1, d), lambda i, j: (0, 0)),
                  pl.BlockSpec((d, tn), lambda i, j: (0, j)),
                  pl.BlockSpec((d, LANES), lambda i, j: (0, 0))],
        out_specs=[pl.BlockSpec((tm, tn), lambda i, j: (i, j)),
                   pl.BlockSpec((tm, LANES), lambda i, j: (i, 0))],
        out_shape=[SDS((m, n), F32), SDS((m, LANES), F32)],
        scratch_shapes=[pltpu.VMEM((tm, d), BF16)],
        compiler_params=_cparams(("parallel", "arbitrary"),
                                 [((tm, d), F32, 3), ((tm, d), BF16, 1), ((d, tn), BF16, 2),
                                  ((tm, tn), F32, 3), ((tm, LANES), F32, 2), ((d, LANES), BF16, 2)]),
        name="in_proj",
    )(x, g, w, wdt)


_NT = (((1,), (1,)), ((), ()))
LOG2E = 1.4426950408889634


def _sb_logs(z):
    neg_abs = lax.bitcast_convert_type(lax.bitcast_convert_type(z, jnp.uint32) | jnp.uint32(0x80000000), F32)
    lbeta = jnp.minimum(z, 0.0) - jnp.log2(1.0 + jnp.exp2(neg_abs))
    return lbeta, lbeta - z


def _sb_update(qm, kblk, vblk, bias, uo, carry, acc, mask):
    z = lax.dot_general(qm, kblk, _NT, preferred_element_type=F32) + bias
    lbeta, lk = _sb_logs(z)
    lk = jnp.where(mask, lk, 0.0)
    r = jnp.dot(lk.astype(BF16), uo, preferred_element_type=F32)
    w = jnp.where(mask, jnp.exp2(lbeta + r[:, :TK_ATT] + carry), 0.0)
    acc = acc + jnp.dot(w.astype(BF16), vblk, preferred_element_type=F32)
    return carry + r[:, TK_ATT:], acc


def _cumsum_matrix():
    jp = lax.broadcasted_iota(jnp.int32, (TK_ATT, 2 * TK_ATT), 0)
    j = lax.broadcasted_iota(jnp.int32, (TK_ATT, 2 * TK_ATT), 1)
    return jnp.where((jp > j) | (j >= TK_ATT), 1.0, 0.0).astype(BF16)


def _att_prompt_body(bias_ref, q_ref, k_ref, v_ref, g_ref, uo_ref, o_ref, kt_ref, vt_ref, carry_ref, acc_ref,
                     *, tq, scale):
    hp = pl.program_id(1)
    qi = pl.program_id(2)

    @pl.when(qi == 0)
    def _():
        for blk in range(k_ref.shape[0] // LANES):
            rows = slice(blk * LANES, (blk + 1) * LANES)
            kt_ref[0, :, rows] = k_ref[rows, :].T
            vt_ref[0, :, rows] = v_ref[rows, :].T

    nsub = tq // TK_ATT
    lane = lax.broadcasted_iota(jnp.int32, (1, LANES), 1)
    low = lane < D_HEAD_ATT
    uo = uo_ref[...]
    qb = (q_ref[...] * (scale * LOG2E)).astype(BF16)
    lane2 = lax.broadcasted_iota(jnp.int32, (1, 2 * TK_ATT), 1)
    bias = jnp.where(lane2 < TK_ATT, bias_ref[2 * hp], bias_ref[2 * hp + 1]) * LOG2E
    qrow = lax.broadcasted_iota(jnp.int32, (tq, 1), 0)
    carry_ref[...] = jnp.zeros_like(carry_ref)
    acc_ref[...] = jnp.zeros_like(acc_ref)

    def span_step(ks, nblk, masked):
        k0 = pl.multiple_of(ks * tq, tq)
        carries = [carry_ref[0], carry_ref[1]]
        accs = [acc_ref[0], acc_ref[1]]
        order = list(reversed(range(nblk)))
        zs, vblks, masks = [], [], []
        for sb in order:
            rows = pl.ds(k0 + sb * TK_ATT, TK_ATT)
            kf = k_ref[rows, :]
            kbd = jnp.concatenate([jnp.where(low, kf, 0.0), jnp.where(low, 0.0, kf)], axis=0).astype(BF16)
            vblks.append(v_ref[rows, :].astype(BF16))
            zs.append(lax.dot_general(qb, kbd, _NT, preferred_element_type=F32) + bias)
            masks.append((sb * TK_ATT + (lane2 & (TK_ATT - 1)) < qrow) if masked else None)
        lbetas, rs = [], []
        for z, mask in zip(zs, masks):
            lbeta, lk = _sb_logs(z)
            if masked:
                lk = jnp.where(mask, lk, 0.0)
            lkb = lk.astype(BF16)
            lbetas.append(lbeta)
            rs.append([jnp.dot(lkb[:, hh * TK_ATT:(hh + 1) * TK_ATT], uo, preferred_element_type=F32)
                       for hh in range(2)])
        for lbeta, r2, mask, vblk in zip(lbetas, rs, masks, vblks):
            for hh in range(2):
                hs = slice(hh * TK_ATT, (hh + 1) * TK_ATT)
                w = jnp.exp2(lbeta[:, hs] + r2[hh][:, :TK_ATT] + carries[hh])
                if masked:
                    w = jnp.where(mask[:, hs], w, 0.0)
                accs[hh] = accs[hh] + jnp.dot(w.astype(BF16), vblk, preferred_element_type=F32)
                carries[hh] = carries[hh] + r2[hh][:, TK_ATT:]
        for hh in range(2):
            carry_ref[hh] = carries[hh]
            acc_ref[hh] = accs[hh]

    span_step(qi, nsub, True)

    @pl.when((qi & 1) == 1)
    def _():
        span_step(qi - 1, nsub, False)

    nwide = lax.shift_right_logical(qi, 1)

    def off_diag(t, _):
        span_step(2 * (nwide - 1 - t), 2 * nsub, False)
        return 0

    lax.fori_loop(0, nwide, off_diag, 0)
    o = jnp.where(low, acc_ref[0], acc_ref[1])
    sq = o * o
    s0 = jnp.sum(jnp.where(low, sq, 0.0), axis=-1, keepdims=True)
    s1 = jnp.sum(jnp.where(low, 0.0, sq), axis=-1, keepdims=True)
    ms = jnp.where(low, s0, s1) * (1.0 / D_HEAD_ATT)
    o_ref[...] = ((o * lax.rsqrt(ms + EPS)) * g_ref[...]).astype(o_ref.dtype)


def _att_prompt(proj, bias, g, uo, nb, seq):
    m = proj.shape[0]
    tq = _tile(seq, 256)
    nq = seq // tq
    npair = H_ATT // 2
    body = functools.partial(_att_prompt_body, tq=tq, scale=D_HEAD_ATT ** -0.5)
    return pl.pallas_call(
        body,
        grid=(nb, npair, nq),
        in_specs=[pl.BlockSpec(memory_space=pltpu.SMEM),
                  pl.BlockSpec((tq, LANES), lambda b, h, i: (b * nq + i, COL_Q // LANES + h)),
                  pl.BlockSpec((seq, LANES), lambda b, h, i: (b, COL_K // LANES + h)),
                  pl.BlockSpec((seq, LANES), lambda b, h, i: (b, COL_V // LANES + h)),
                  pl.BlockSpec((1, LANES), lambda b, h, i: (0, h)),
                  pl.BlockSpec((TK_ATT, 2 * TK_ATT), lambda b, h, i: (0, 0))],
        out_specs=[pl.BlockSpec((tq, LANES), lambda b, h, i: (b * nq + i, h)),
                   pl.BlockSpec((1, LANES, seq), lambda b, h, i: (b, h, 0)),
                   pl.BlockSpec((1, LANES, seq), lambda b, h, i: (b, h, 0))],
        out_shape=[SDS((m, D_ATT), BF16), SDS((nb, D_ATT, seq), F32), SDS((nb, D_ATT, seq), F32)],
        scratch_shapes=[pltpu.VMEM((2, tq, TK_ATT), F32), pltpu.VMEM((2, tq, LANES), F32)],
        compiler_params=_cparams(("parallel", "parallel", "arbitrary"),
                                 [((seq, LANES), F32, 8), ((tq, LANES), F32, 24)]),
        name="att_prompt",
    )(bias, proj, proj, proj, g, uo)


def _att_sample_body(pt_ref, q_ref, kn_ref, vn_ref, bias_ref, g_ref, uo_ref, *rest, npg, scale):
    k_refs, v_refs = rest[:npg], rest[npg:2 * npg]
    o_ref, qbd_ref, kpad_ref, vpad_ref, carry_ref, acc_ref = rest[2 * npg:]
    del pt_ref
    p = pl.program_id(1)
    nrow = H_ATT * SUBLANES
    row = lax.broadcasted_iota(jnp.int32, (nrow, D_ATT), 0)
    lane = lax.broadcasted_iota(jnp.int32, (nrow, D_ATT), 1)
    own = (row >> 3) == (lane >> 6)
    uo = uo_ref[...]
    bias = bias_ref[...] * LOG2E

    @pl.when(p == 0)
    def _():
        q = q_ref[...] * (scale * LOG2E)
        qt = jnp.concatenate([q] * H_ATT, axis=0)
        qbd = jnp.where(own, qt, 0.0).astype(BF16)
        qbd_ref[...] = qbd
        kpad_ref[...] = jnp.zeros_like(kpad_ref)
        vpad_ref[...] = jnp.zeros_like(vpad_ref)
        kpad_ref[0:SUBLANES, :] = kn_ref[...]
        vpad_ref[0:SUBLANES, :] = vn_ref[...]
        tok = lax.broadcasted_iota(jnp.int32, (nrow, 1), 0) & (SUBLANES - 1)
        mask = lax.broadcasted_iota(jnp.int32, (1, TK_ATT), 1) < tok
        carry, acc = _sb_update(qbd, kpad_ref[...].astype(BF16), vpad_ref[...].astype(BF16), bias, uo,
                                jnp.zeros((nrow, TK_ATT), F32), jnp.zeros((nrow, D_ATT), F32), mask)
        carry_ref[...] = carry
        acc_ref[...] = acc

    carry, acc = carry_ref[...], acc_ref[...]
    qbd = qbd_ref[...]
    zs = [jnp.dot(qbd, k_refs[gi][0].astype(BF16), preferred_element_type=F32) + bias for gi in range(npg)]
    lbetas, rs = [], []
    for z in zs:
        lbeta, lk = _sb_logs(z)
        lbetas.append(lbeta)
        rs.append(jnp.dot(lk.astype(BF16), uo, preferred_element_type=F32))
    for gi in range(npg):
        w = jnp.exp2(lbetas[gi] + rs[gi][:, :TK_ATT] + carry)
        acc = acc + lax.dot_general(w.astype(BF16), v_refs[gi][0].astype(BF16), _NT, preferred_element_type=F32)
        carry = carry + rs[gi][:, TK_ATT:]
    carry_ref[...] = carry
    acc_ref[...] = acc

    @pl.when(p == pl.num_programs(1) - 1)
    def _():
        a = jnp.where(own, acc, 0.0)
        ms = jnp.sum(a * a, axis=-1, keepdims=True) * (1.0 / D_HEAD_ATT)
        a = a * lax.rsqrt(ms + EPS)
        o = a[0:SUBLANES]
        for h in range(1, H_ATT):
            o = o + a[h * SUBLANES:(h + 1) * SUBLANES]
        o_ref[...] = o * g_ref[...]


def _att_sample(proj, cache_k, cache_v, page_table, layer, n_pool, bias_rows, g, uo):
    m = proj.shape[0]
    nseq, npages = page_table.shape
    assert m == nseq * SUBLANES
    npg = next(n for n in (8, 4, 2, 1) if npages % n == 0)
    pt = page_table.reshape(-1)
    ck = jnp.transpose(cache_k, (0, 1, 3, 4, 2)).reshape(-1, D_ATT, PAGE_SIZE)
    cv = jnp.transpose(cache_v, (0, 1, 3, 4, 2)).reshape(-1, D_ATT, PAGE_SIZE)
    nrow = H_ATT * SUBLANES

    def page_map(gi):
        def f(b, p, pt_ref):
            return (layer * n_pool + pt_ref[b * npages + (npages - 1 - (p * npg + gi))], 0, 0)
        return f

    page_specs = [pl.BlockSpec((1, D_ATT, PAGE_SIZE), page_map(gi)) for gi in range(npg)]
    body = functools.partial(_att_sample_body, npg=npg, scale=D_HEAD_ATT ** -0.5)
    grid_spec = pltpu.PrefetchScalarGridSpec(
        num_scalar_prefetch=1,
        grid=(nseq, npages // npg),
        in_specs=[pl.BlockSpec((SUBLANES, D_ATT), lambda b, p, pt_ref: (b, COL_Q // D_ATT)),
                  pl.BlockSpec((SUBLANES, D_ATT), lambda b, p, pt_ref: (b, COL_K // D_ATT)),
                  pl.BlockSpec((SUBLANES, D_ATT), lambda b, p, pt_ref: (b, COL_V // D_ATT)),
                  pl.BlockSpec((nrow, TK_ATT), lambda b, p, pt_ref: (0, 0)),
                  pl.BlockSpec((1, D_ATT), lambda b, p, pt_ref: (0, 0)),
                  pl.BlockSpec((TK_ATT, 2 * TK_ATT), lambda b, p, pt_ref: (0, 0))] + page_specs + page_specs,
        out_specs=pl.BlockSpec((SUBLANES, D_ATT), lambda b, p, pt_ref: (b, 0)),
        scratch_shapes=[pltpu.VMEM((nrow, D_ATT), BF16),
                        pltpu.VMEM((TK_ATT, D_ATT), F32),
                        pltpu.VMEM((TK_ATT, D_ATT), F32),
                        pltpu.VMEM((nrow, TK_ATT), F32),
                        pltpu.VMEM((nrow, D_ATT), F32)])
    return pl.pallas_call(
        body,
        grid_spec=grid_spec,
        out_shape=SDS((m, D_ATT), F32),
        compiler_params=_cparams(("parallel", "arbitrary"),
                                 [((PAGE_SIZE, D_ATT), F32, 4 * npg), ((TK_ATT, D_ATT), F32, 4)]),
        name="att_sample",
    )(pt, proj, proj, proj, bias_rows, g, uo, *([ck] * npg), *([cv] * npg))


def _ssd_body(*refs, lr, nc, has_h0):
    if has_h0:
        (z_ref, xs_ref, bc_ref, dt_ref, c0_ref, h0_ref, cw_ref, cb_ref, dtb_ref, alog_ref, dd_ref, ng_ref,
         tri_ref, y_ref, ht_ref, win_ref, hs_ref, dts_ref, ysc_ref, xd_ref, et_ref) = refs
    else:
        (z_ref, xs_ref, bc_ref, dt_ref, c0_ref, cw_ref, cb_ref, dtb_ref, alog_ref, dd_ref, ng_ref,
         tri_ref, y_ref, ht_ref, win_ref, hs_ref, dts_ref, ysc_ref, xd_ref, et_ref) = refs
        h0_ref = None
    L = SSD_CHUNK
    c = pl.program_id(1)
    hist = SUBLANES
    pairs_per_group = H_SSD // G_SSD // 2

    @pl.when(c == 0)
    def _():
        win_ref[0:hist, :] = c0_ref[0]
        if lr < L:
            win_ref[hist:hist + L, :] = jnp.zeros((L, SSD_CONV_DIM), F32)
            dts_ref[...] = jnp.zeros_like(dts_ref)
        if has_h0:
            for hp in range(H_SSD // 2):
                pair = jnp.concatenate([h0_ref[0, 2 * hp], h0_ref[0, 2 * hp + 1]], axis=0)
                gsl = slice((hp % pairs_per_group) * LANES, (hp % pairs_per_group + 1) * LANES)
                hs_ref[hp // pairs_per_group, :, gsl] = pair.T
        else:
            hs_ref[...] = jnp.zeros_like(hs_ref)

    win_ref[hist:hist + lr, 0:D_SSD] = xs_ref[...]
    win_ref[hist:hist + lr, D_SSD:SSD_CONV_DIM] = bc_ref[...]
    dts_ref[0:lr, :] = dt_ref[...]

    lane = lax.broadcasted_iota(jnp.int32, (1, LANES), 1)
    low = lane < P_SSD
    rowi = lax.broadcasted_iota(jnp.int32, (L, 1), 0)
    dt = _softplus(dts_ref[...] + dtb_ref[...])
    if lr < L:
        dt = jnp.where(rowi < lr, dt, 0.0)
    a_head = jnp.where(lane < H_SSD, -jnp.exp(alog_ref[...]), 0.0)
    a = dt * a_head
    tri = tri_ref[...]
    a1, a2, a3 = _split3(a)
    a_cs = (jnp.dot(tri, a1, preferred_element_type=F32) + jnp.dot(tri, a2, preferred_element_type=F32)
            + jnp.dot(tri, a3, preferred_element_type=F32))
    a_cs_t = a_cs.T
    a_tot = a_cs[L - 1:L, :]
    causal = rowi >= lax.broadcasted_iota(jnp.int32, (1, L), 1)

    pre = cb_ref[...]
    for k in range(SSD_CONV_W):
        off = hist - (SSD_CONV_W - 1) + k
        pre = pre + cw_ref[k:k + 1, :] * win_ref[off:off + L, :]
    xbc = pre * _sigmoid(pre)
    win_ref[0:hist, :] = win_ref[L:L + hist, :]

    cbs, bts, cgs = [], [], []
    for g in range(G_SSD):
        bg = xbc[:, D_SSD + g * N_SSD:D_SSD + (g + 1) * N_SSD]
        cg = xbc[:, D_SSD + (G_SSD + g) * N_SSD:D_SSD + (G_SSD + g + 1) * N_SSD].astype(BF16)
        cbs.append(lax.dot_general(cg, bg.astype(BF16), (((1,), (1,)), ((), ())), preferred_element_type=F32))
        bts.append(bg.T.astype(BF16))
        cgs.append(cg)

    for hp in range(H_SSD // 2):
        g = hp // pairs_per_group
        cols, dtcols, mats = [], [], []
        for h in (2 * hp, 2 * hp + 1):
            col = jnp.broadcast_to(a_cs[:, h:h + 1], (L, LANES))
            dec = jnp.exp(jnp.where(causal, col - a_cs_t[h:h + 1, :], -1e30))
            mats.append((cbs[g] * dec).astype(BF16))
            cols.append(col)
            dtcols.append(jnp.broadcast_to(dt[:, h:h + 1], (L, LANES)))
        sl = slice(hp * LANES, (hp + 1) * LANES)
        xs_pair = xbc[:, sl]
        xdt = xs_pair * jnp.where(low, dtcols[0], dtcols[1])
        xdt_bf = xdt.astype(BF16)
        ydiag = jnp.where(low, jnp.dot(mats[0], xdt_bf, preferred_element_type=F32),
                          jnp.dot(mats[1], xdt_bf, preferred_element_type=F32))
        cs_pair = jnp.where(low, cols[0], cols[1])
        gsl = slice((hp % pairs_per_group) * LANES, (hp % pairs_per_group + 1) * LANES)
        yoff = jnp.dot(cgs[g], hs_ref[g, :, gsl].astype(BF16), preferred_element_type=F32) * jnp.exp(cs_pair)
        ysc_ref[:, sl] = ydiag + yoff + dd_ref[:, sl] * xs_pair
        tot_pair = jnp.where(low, jnp.broadcast_to(a_tot[:, 2 * hp:2 * hp + 1], (1, LANES)),
                             jnp.broadcast_to(a_tot[:, 2 * hp + 1:2 * hp + 2], (1, LANES)))
        xd_ref[:, sl] = (xdt * jnp.exp(tot_pair - cs_pair)).astype(BF16)
        et_ref[:, sl] = jnp.exp(tot_pair)

    gw = D_SSD // G_SSD
    for g in range(G_SSD):
        gs = slice(g * gw, (g + 1) * gw)
        cst = jnp.dot(bts[g], xd_ref[:, gs], preferred_element_type=F32)
        hs_ref[g] = hs_ref[g] * et_ref[:, gs] + cst

    zz = z_ref[...]
    y = ysc_ref[0:lr, :] * (zz * _sigmoid(zz))
    for g in range(G_SSD):
        gs = slice(g * gw, (g + 1) * gw)
        yg = y[:, gs]
        ms = jnp.mean(yg * yg, axis=-1, keepdims=True)
        y_ref[:, gs] = ((yg * lax.rsqrt(ms + EPS)) * ng_ref[:, gs]).astype(y_ref.dtype)

    @pl.when(c == nc - 1)
    def _():
        for hp in range(H_SSD // 2):
            gsl = slice((hp % pairs_per_group) * LANES, (hp % pairs_per_group + 1) * LANES)
            pair_t = hs_ref[hp // pairs_per_group, :, gsl].T
            ht_ref[0, 2 * hp] = pair_t[0:P_SSD]
            ht_ref[0, 2 * hp + 1] = pair_t[P_SSD:2 * P_SSD]


def _ssd(proj, dtp, conv0, h0, cw, cb, dtb, alog, dd, ng, tri, nb, seq, out_dtype):
    m = proj.shape[0]
    lr = min(seq, SSD_CHUNK)
    assert seq % lr == 0
    nc = seq // lr
    has_h0 = h0 is not None
    gw = D_SSD // G_SSD
    row = lambda b, c: b * nc + c
    in_specs = [pl.BlockSpec((lr, D_SSD), lambda b, c: (row(b, c), COL_Z // D_SSD)),
                pl.BlockSpec((lr, D_SSD), lambda b, c: (row(b, c), COL_XS // D_SSD)),
                pl.BlockSpec((lr, 2 * G_SSD * N_SSD), lambda b, c: (row(b, c), COL_BC // (2 * G_SSD * N_SSD))),
                pl.BlockSpec((lr, LANES), lambda b, c: (row(b, c), 0)),
                pl.BlockSpec((1, SUBLANES, SSD_CONV_DIM), lambda b, c: (b, 0, 0))]
    args = [proj, proj, proj, dtp, conv0]
    if has_h0:
        in_specs.append(pl.BlockSpec((1, H_SSD, P_SSD, N_SSD), lambda b, c: (b, 0, 0, 0)))
        args.append(h0)
    const = lambda shape: pl.BlockSpec(shape, lambda b, c: (0,) * len(shape))
    in_specs += [const((SSD_CONV_W, SSD_CONV_DIM)), const((1, SSD_CONV_DIM)), const((1, LANES)),
                 const((1, LANES)), const((1, D_SSD)), const((1, D_SSD)), const((SSD_CHUNK, SSD_CHUNK))]
    args += [cw, cb, dtb, alog, dd, ng, tri]
    body = functools.partial(_ssd_body, lr=lr, nc=nc, has_h0=has_h0)
    L = SSD_CHUNK
    return pl.pallas_call(
        body,
        grid=(nb, nc),
        in_specs=in_specs,
        out_specs=[pl.BlockSpec((lr, D_SSD), lambda b, c: (row(b, c), 0)),
                   pl.BlockSpec((1, H_SSD, P_SSD, N_SSD), lambda b, c: (b, 0, 0, 0))],
        out_shape=[SDS((m, D_SSD), out_dtype), SDS((nb, H_SSD, P_SSD, N_SSD), F32)],
        scratch_shapes=[pltpu.VMEM((L + SUBLANES, SSD_CONV_DIM), F32),
                        pltpu.VMEM((G_SSD, N_SSD, gw), F32),
                        pltpu.VMEM((L, LANES), F32),
                        pltpu.VMEM((L, D_SSD), F32),
                        pltpu.VMEM((L, D_SSD), BF16),
                        pltpu.VMEM((1, D_SSD), F32)],
        compiler_params=_cparams(("parallel", "arbitrary"),
                                 [((L, SSD_CONV_DIM), F32, 12), ((G_SSD, N_SSD, gw), F32, 5)]),
        name="ssd",
    )(*args)


def _conf_body(glu_ref, c0_ref, w_ref, b_ref, lg_ref, lb_ref, o_ref, tail_ref, win_ref, part_ref, *, tt, nt):
    t = pl.program_id(1)

    @pl.when(t == 0)
    def _():
        win_ref[0:CONF_PAD, :] = c0_ref[0]

    glu = glu_ref[...]
    win_ref[CONF_PAD:CONF_PAD + tt, :] = glu[:, :C_CONF] * _sigmoid(glu[:, C_CONF:])
    first = CONF_PAD - (CONF_W - 1)
    acc = jnp.broadcast_to(b_ref[...], (tt, C_CONF))
    for r in range(SUBLANES):
        offs = [j for j in range(r, first + CONF_W, SUBLANES) if j >= first]
        rows = tt if r == 0 else tt + SUBLANES
        part = w_ref[offs[0] - first:offs[0] - first + 1, :] * win_ref[offs[0] - r:offs[0] - r + rows, :]
        for j in offs[1:]:
            part = part + w_ref[j - first:j - first + 1, :] * win_ref[j - r:j - r + rows, :]
        if r == 0:
            acc = acc + part
        else:
            part_ref[r - 1] = part
            acc = acc + part_ref[r - 1, r:r + tt, :]
    mu = jnp.mean(acc, axis=-1, keepdims=True)
    xc = acc - mu
    var = jnp.mean(xc * xc, axis=-1, keepdims=True)
    yn = (xc * lax.rsqrt(var + EPS)) * lg_ref[...] + lb_ref[...]
    o_ref[...] = (yn * _sigmoid(yn)).astype(o_ref.dtype)
    tail = win_ref[tt:tt + CONF_PAD, :]
    tail_ref[0] = tail
    if nt > 1:
        win_ref[0:CONF_PAD, :] = tail


def _conformer(proj, conv0, w, b, lg, lb, nb, seq, out_dtype):
    m = proj.shape[0]
    tt = _tile(seq, 256)
    nt = seq // tt
    assert nt == 1 or tt >= CONF_PAD
    body = functools.partial(_conf_body, tt=tt, nt=nt)
    const = lambda shape: pl.BlockSpec(shape, lambda bb, t: (0,) * len(shape))
    return pl.pallas_call(
        body,
        grid=(nb, nt),
        in_specs=[pl.BlockSpec((tt, 2 * C_CONF), lambda bb, t: (bb * nt + t, COL_GLU // (2 * C_CONF))),
                  pl.BlockSpec((1, CONF_PAD, C_CONF), lambda bb, t: (bb, 0, 0)),
                  const((CONF_W, C_CONF)), const((1, C_CONF)), const((1, C_CONF)), const((1, C_CONF))],
        out_specs=[pl.BlockSpec((tt, C_CONF), lambda bb, t: (bb * nt + t, 0)),
                   pl.BlockSpec((1, CONF_PAD, C_CONF), lambda bb, t: (bb, 0, 0))],
        out_shape=[SDS((m, C_CONF), out_dtype), SDS((nb, CONF_PAD, C_CONF), F32)],
        scratch_shapes=[pltpu.VMEM((CONF_PAD + tt, C_CONF), F32),
                        pltpu.VMEM((SUBLANES - 1, tt + SUBLANES, C_CONF), F32)],
        compiler_params=_cparams(("parallel", "arbitrary"),
                                 [((tt, 2 * C_CONF), F32, 8), ((SUBLANES, tt + SUBLANES, C_CONF), F32, 1)]),
        name="conformer",
    )(proj, conv0, w, b, lg, lb)


def _outproj_body(x_ref, a_ref, y_ref, u_ref, w_ref, g_ref, x1_ref, h_ref):
    acc = x_ref[...]
    acc = acc + jnp.dot(a_ref[...].astype(BF16), w_ref[0:D_ATT, :], preferred_element_type=F32)
    acc = acc + jnp.dot(y_ref[...].astype(BF16), w_ref[D_ATT:D_ATT + D_SSD, :], preferred_element_type=F32)
    acc = acc + jnp.dot(u_ref[...].astype(BF16), w_ref[D_ATT + D_SSD:, :], preferred_element_type=F32)
    x1_ref[...] = acc
    ms = jnp.mean(acc * acc, axis=-1, keepdims=True)
    h_ref[...] = ((acc * lax.rsqrt(ms + EPS)) * g_ref[...]).astype(BF16)


def _out_proj(x, att, y, u, w, g):
    m, d = x.shape
    tm = _tile(m, 512)
    row = lambda width: pl.BlockSpec((tm, width), lambda i: (i, 0))
    return pl.pallas_call(
        _outproj_body,
        grid=(m // tm,),
        in_specs=[row(d), row(D_ATT), row(D_SSD), row(C_CONF),
                  pl.BlockSpec((d, d), lambda i: (0, 0)), pl.BlockSpec((1, d), lambda i: (0, 0))],
        out_specs=[row(d), row(d)],
        out_shape=[SDS((m, d), F32), SDS((m, d), BF16)],
        compiler_params=_cparams(("parallel",), [((tm, d), F32, 7), ((d, d), BF16, 2)]),
        name="out_proj",
    )(x, att, y, u, w, g)


HALO = 16


def _ffn_up_body(*refs, tm, seq, sample):
    if sample:
        h_ref, wg_ref, wv_ref, cw_ref, cb_ref, s1_ref, s2_ref, act_ref, tail_ref, lhs_ref, gs_ref = refs
    else:
        h_ref, halo_ref, wg_ref, wv_ref, cw_ref, cb_ref, act_ref, tail_ref, lhs_ref, gs_ref = refs
    i = pl.program_id(0)

    @pl.when(pl.program_id(1) == 0)
    def _():
        lhs_ref[HALO:, :] = h_ref[...]
        if sample:
            lhs_ref[0:HALO, :] = jnp.zeros((HALO, D_MODEL), BF16)
        else:
            lhs_ref[0:HALO, :] = halo_ref[...]

            @pl.when((i * tm) % seq == 0)
            def _():
                lhs_ref[0:HALO, :] = jnp.zeros((HALO, D_MODEL), BF16)

    gs_ref[...] = jnp.dot(lhs_ref[...], wg_ref[...], preferred_element_type=F32)
    val = jnp.dot(lhs_ref[HALO:, :], wv_ref[...], preferred_element_type=F32)
    g0 = gs_ref[HALO:HALO + tm, :]
    g1 = gs_ref[HALO - 1:HALO - 1 + tm, :]
    g2 = gs_ref[HALO - 2:HALO - 2 + tm, :]
    if sample:
        tok = lax.broadcasted_iota(jnp.int32, (tm, 1), 0) & (seq - 1)
        g1 = jnp.where(tok < 1, s1_ref[...], g1)
        g2 = jnp.where(tok < 2, s2_ref[...], g2)
        tail_ref[...] = g0
    else:
        tail_ref[...] = gs_ref[HALO + tm - SUBLANES:HALO + tm, :]
    pre = cw_ref[0:1, :] * g2 + cw_ref[1:2, :] * g1 + cw_ref[2:3, :] * g0 + cb_ref[...]
    act_ref[...] = ((pre * _sigmoid(pre)) * val).astype(BF16)


def _ffn_up(h, wg, wv, cw, cb, seq, s1=None, s2=None):
    m, d = h.shape
    f = wg.shape[1]
    sample = s1 is not None
    tm, tn = _tile(m, m if sample else min(1024, seq)), FFN_TN
    if sample:
        assert seq & (seq - 1) == 0 and seq >= FFN_CONV_W - 1
    else:
        assert seq % tm == 0
    body = functools.partial(_ffn_up_body, tm=tm, seq=seq, sample=sample)
    hspec = pl.BlockSpec((tm, d), lambda i, j: (i, 0))
    wspec = pl.BlockSpec((d, tn), lambda i, j: (0, j))
    cspec = lambda r: pl.BlockSpec((r, tn), lambda i, j: (0, j))
    tile = pl.BlockSpec((tm, tn), lambda i, j: (i, j))
    if sample:
        in_specs = [hspec, wspec, wspec, cspec(FFN_CONV_W), cspec(1), tile, tile]
        args = (h, wg, wv, cw, cb, s1, s2)
        tail_rows, tail_spec = m, tile
    else:
        halo = pl.BlockSpec((HALO, d), lambda i, j: (jnp.maximum(i * (tm // HALO) - 1, 0), 0))
        in_specs = [hspec, halo, wspec, wspec, cspec(FFN_CONV_W), cspec(1)]
        args = (h, h, wg, wv, cw, cb)
        tail_rows, tail_spec = (m // tm) * SUBLANES, pl.BlockSpec((SUBLANES, tn), lambda i, j: (i, j))
    return pl.pallas_call(
        body,
        grid=(m // tm, pl.cdiv(f, tn)),
        in_specs=in_specs,
        out_specs=[tile, tail_spec],
        out_shape=[SDS((m, f), BF16), SDS((tail_rows, f), F32)],
        scratch_shapes=[pltpu.VMEM((HALO + tm, d), BF16), pltpu.VMEM((HALO + tm, tn), F32)],
        compiler_params=_cparams(("parallel", "arbitrary"),
                                 [((tm, d), BF16, 3), ((d, tn), BF16, 4), ((tm, tn), F32, 8)]),
        name="ffn_up",
    )(*args)


def _ffn_down_body(x_ref, a_ref, w_ref, o_ref):
    o_ref[...] = x_ref[...] + jnp.dot(a_ref[...], w_ref[...], preferred_element_type=F32)


def _ffn_down(x, act, w):
    m, d = x.shape
    f = act.shape[1]
    tm, tn = _tile(m, 1024), _tile(d, 512)
    return pl.pallas_call(
        _ffn_down_body,
        grid=(m // tm, d // tn),
        in_specs=[pl.BlockSpec((tm, tn), lambda i, j: (i, j)),
                  pl.BlockSpec((tm, f), lambda i, j: (i, 0)),
                  pl.BlockSpec((f, tn), lambda i, j: (0, j))],
        out_specs=pl.BlockSpec((tm, tn), lambda i, j: (i, j)),
        out_shape=SDS((m, d), F32),
        compiler_params=_cparams(("parallel", "arbitrary"),
                                 [((tm, f), BF16, 2), ((f, tn), BF16, 2), ((tm, tn), F32, 6)]),
        name="ffn_down",
    )(x, act, w)


def _final_norm_body(x_ref, g_ref, o_ref):
    x = x_ref[...]
    ms = jnp.mean(x * x, axis=-1, keepdims=True)
    o_ref[...] = (x * lax.rsqrt(ms + EPS)) * g_ref[...]


def _final_norm(x, g):
    m, d = x.shape
    tm = _tile(m, 512)
    return pl.pallas_call(
        _final_norm_body,
        grid=(m // tm,),
        in_specs=[pl.BlockSpec((tm, d), lambda i: (i, 0)), pl.BlockSpec((1, d), lambda i: (0, 0))],
        out_specs=pl.BlockSpec((tm, d), lambda i: (i, 0)),
        out_shape=SDS((m, d), F32),
        compiler_params=_cparams(("parallel",), [((tm, d), F32, 6)]),
        name="final_norm",
    )(x, g)


def _pad_cols(x, width):
    return jnp.pad(x, ((0, 0), (0, width - x.shape[1])))


def _layer_weights(l, p):
    w_in = p["w_in"][l]
    c_z, c_x, c_dt = 3 * D_ATT, 3 * D_ATT + D_SSD, 3 * D_ATT + D_SSD + SSD_CONV_DIM
    c_glu = c_dt + H_SSD
    order = [w_in[:, c_z:c_z + D_SSD],
             w_in[:, c_x:c_x + D_SSD],
             w_in[:, c_glu:c_glu + 2 * C_CONF],
             w_in[:, c_x + D_SSD:c_dt],
             w_in[:, 0:3 * D_ATT]]
    w_up = p["w_up"][l]
    row = lambda v: v.reshape(1, -1)
    return dict(
        w_main=jnp.concatenate(order, axis=1).astype(BF16),
        w_dt=_pad_cols(w_in[:, c_dt:c_dt + H_SSD], LANES).astype(BF16),
        g_mix=row(p["norm_mix_g"][l]),
        att_bias=p["att_logit_bias"][l],
        att_g=row(p["att_norm_g"][l]),
        ssd_cw=p["ssd_conv_w"][l], ssd_cb=row(p["ssd_conv_b"][l]),
        dtb=_pad_cols(row(p["ssd_dt_bias"][l]), LANES), alog=_pad_cols(row(p["ssd_a_log"][l]), LANES),
        dd=row(jnp.repeat(p["ssd_d"][l], P_SSD)), ssd_ng=row(p["ssd_norm_g"][l]),
        conf_w=p["conf_conv_w"][l], conf_b=row(p["conf_conv_b"][l]),
        conf_lg=row(p["conf_ln_g"][l]), conf_lb=row(p["conf_ln_b"][l]),
        w_out=p["w_out"][l].astype(BF16), g_ffn=row(p["norm_ffn_g"][l]),
        w_gate=w_up[:, :D_FF].astype(BF16), w_val=w_up[:, D_FF:].astype(BF16),
        ffn_cw=p["ffn_conv_w"][l], ffn_cb=row(p["ffn_conv_b"][l]),
        w_down=p["w_down"][l].astype(BF16),
    )


def _xbc_cols(proj3):
    return jnp.concatenate([proj3[..., COL_XS:COL_XS + D_SSD], proj3[..., COL_BC:COL_BC + 2 * G_SSD * N_SSD]],
                           axis=-1)


def kernel(x_prompt, x_sample, cache_k, cache_v, state_ssm, state_ssd_conv, state_conf_conv, state_ffn_conv, page_table, norm_mix_g, w_in, att_logit_bias, att_norm_g, ssd_conv_w, ssd_conv_b, ssd_dt_bias, ssd_a_log, ssd_d, ssd_norm_g, conf_conv_w, conf_conv_b, conf_ln_g, conf_ln_b, w_out, norm_ffn_g, w_up, ffn_conv_w, ffn_conv_b, w_down, norm_final_g):
    params = dict(norm_mix_g=norm_mix_g, w_in=w_in, att_logit_bias=att_logit_bias, att_norm_g=att_norm_g,
                  ssd_conv_w=ssd_conv_w, ssd_conv_b=ssd_conv_b, ssd_dt_bias=ssd_dt_bias, ssd_a_log=ssd_a_log,
                  ssd_d=ssd_d, ssd_norm_g=ssd_norm_g, conf_conv_w=conf_conv_w, conf_conv_b=conf_conv_b,
                  conf_ln_g=conf_ln_g, conf_ln_b=conf_ln_b, w_out=w_out, norm_ffn_g=norm_ffn_g, w_up=w_up,
                  ffn_conv_w=ffn_conv_w, ffn_conv_b=ffn_conv_b, w_down=w_down)
    depth = w_in.shape[0]
    bp, sp, d = x_prompt.shape
    bs, ss, _ = x_sample.shape
    assert ss == SUBLANES and d == D_MODEL
    n_pool = cache_k.shape[1]
    xp = x_prompt.reshape(bp * sp, d)
    xs = x_sample.reshape(bs * ss, d)
    uo = _cumsum_matrix()
    tri = jnp.tril(jnp.ones((SSD_CHUNK, SSD_CHUNK), F32)).astype(BF16)
    zeros_ssd_conv = jnp.zeros((bp, SUBLANES, SSD_CONV_DIM), F32)
    zeros_conf_conv = jnp.zeros((bp, CONF_PAD, C_CONF), F32)
    outs_p, outs_s = [], []
    for l in range(depth):
        w = _layer_weights(l, params)

        proj, dtp = _in_proj(xp, w["g_mix"], w["w_main"], w["w_dt"])
        att, k_t, v_t = _att_prompt(proj, w["att_bias"], w["att_g"], uo, bp, sp)
        y, h_t = _ssd(proj, dtp, zeros_ssd_conv, None, w["ssd_cw"], w["ssd_cb"], w["dtb"], w["alog"], w["dd"],
                      w["ssd_ng"], tri, bp, sp, BF16)
        u, conf_tail = _conformer(proj, zeros_conf_conv, w["conf_w"], w["conf_b"], w["conf_lg"], w["conf_lb"],
                                  bp, sp, BF16)
        x1, h2 = _out_proj(xp, att, y, u, w["w_out"], w["g_ffn"])
        act, gate_tail = _ffn_up(h2, w["w_gate"], w["w_val"], w["ffn_cw"], w["ffn_cb"], sp)
        xp = _ffn_down(x1, act, w["w_down"])
        proj3 = proj.reshape(bp, sp, D_PROJ)
        tiles_per_seq = gate_tail.shape[0] // SUBLANES // bp
        gt = gate_tail.reshape(bp, tiles_per_seq, SUBLANES, D_FF)
        outs_p.append((
            k_t, v_t,
            h_t,
            _xbc_cols(proj3[:, sp - (SSD_CONV_W - 1):]),
            conf_tail[:, CONF_PAD - (CONF_W - 1):],
            gt[:, -1, SUBLANES - (FFN_CONV_W - 1):]))

        proj, dtp = _in_proj(xs, w["g_mix"], w["w_main"], w["w_dt"])
        bias_rows = jnp.broadcast_to(jnp.repeat(w["att_bias"], SUBLANES)[:, None], (H_ATT * SUBLANES, TK_ATT))
        att = _att_sample(proj, cache_k, cache_v, page_table, l, n_pool, bias_rows, w["att_g"], uo)
        conv0 = jnp.pad(state_ssd_conv[l], ((0, 0), (SUBLANES - (SSD_CONV_W - 1), 0), (0, 0)))
        y, h_t = _ssd(proj, dtp, conv0, state_ssm[l], w["ssd_cw"], w["ssd_cb"], w["dtb"],
                      w["alog"], w["dd"], w["ssd_ng"], tri, bs, ss, F32)
        conf0 = jnp.pad(state_conf_conv[l], ((0, 0), (CONF_PAD - (CONF_W - 1), 0), (0, 0)))
        u, conf_tail = _conformer(proj, conf0, w["conf_w"], w["conf_b"], w["conf_lg"], w["conf_lb"], bs, ss, F32)
        x1, h2 = _out_proj(xs, att, y, u, w["w_out"], w["g_ffn"])
        st = state_ffn_conv[l]
        s1 = jnp.concatenate([st[:, 1:2], jnp.zeros((bs, ss - 1, D_FF), F32)], axis=1).reshape(bs * ss, -1)
        s2 = jnp.concatenate([st, jnp.zeros((bs, ss - 2, D_FF), F32)], axis=1).reshape(bs * ss, -1)
        act, gate_full = _ffn_up(h2, w["w_gate"], w["w_val"], w["ffn_cw"], w["ffn_cb"], ss, s1, s2)
        xs = _ffn_down(x1, act, w["w_down"])
        proj3 = proj.reshape(bs, ss, D_PROJ)
        outs_s.append((
            proj3[..., COL_K:COL_K + D_ATT].reshape(bs, ss, H_ATT, D_HEAD_ATT),
            proj3[..., COL_V:COL_V + D_ATT].reshape(bs, ss, H_ATT, D_HEAD_ATT),
            h_t,
            _xbc_cols(proj3[:, ss - (SSD_CONV_W - 1):]),
            conf_tail[:, CONF_PAD - (CONF_W - 1):],
            gate_full.reshape(bs, ss, D_FF)[:, ss - (FFN_CONV_W - 1):]))

    g_final = norm_final_g.reshape(1, d)
    y_prompt = _final_norm(xp, g_final).reshape(bp, sp, d)
    y_sample = _final_norm(xs, g_final).reshape(bs, ss, d)
    stack = lambda outs, i: jnp.stack([o[i] for o in outs])

    def kv_prompt(i):
        t = stack(outs_p, i).reshape(depth, bp, H_ATT, D_HEAD_ATT, sp)
        return jnp.transpose(t, (0, 1, 4, 2, 3))

    return (y_prompt, y_sample,
            kv_prompt(0), kv_prompt(1), stack(outs_s, 0), stack(outs_s, 1),
            stack(outs_p, 2), stack(outs_s, 2), stack(outs_p, 3), stack(outs_s, 3),
            stack(outs_p, 4), stack(outs_s, 4), stack(outs_p, 5), stack(outs_s, 5))
```

```python
import functools
import math

import jax
import jax.numpy as jnp
from jax import lax
from jax.experimental import pallas as pl
from jax.experimental.pallas import tpu as pltpu

F32 = jnp.float32
BF16 = jnp.bfloat16
SDS = jax.ShapeDtypeStruct

D_MODEL = 2048
D_HEAD_ATT = 64
D_ATT = 512
H_ATT = 8
D_SSD = 1024
P_SSD = 64
H_SSD = 16
G_SSD = 2
N_SSD = 128
SSD_CONV_W = 4
SSD_CONV_DIM = D_SSD + 2 * G_SSD * N_SSD
SSD_CHUNK = 128
C_CONF = 512
CONF_W = 31
D_FF = 5504
FFN_CONV_W = 3
PAGE_SIZE = 128
EPS = 1e-6

LANES = 128
SUBLANES = 8
VMEM_BYTES_V7X = 64 * 1024 * 1024
VMEM_LIMIT_MAX = VMEM_BYTES_V7X - 8 * 1024 * 1024
VMEM_COMPILER_SCRATCH = 16 * 1024 * 1024

D_PROJ = 5120
COL_Z, COL_XS, COL_GLU, COL_BC, COL_Q, COL_K, COL_V = 0, 1024, 2048, 3072, 3584, 4096, 4608
FFN_TN = 512
CONF_PAD = 32
TK_ATT = 128


def _nbytes(shape, dtype):
    return math.prod(shape) * jnp.dtype(dtype).itemsize


def _cparams(sem, bufs):
    est = sum(_nbytes(s, d) * n for s, d, n in bufs)
    return pltpu.CompilerParams(dimension_semantics=sem,
                                vmem_limit_bytes=int(min(est + VMEM_COMPILER_SCRATCH, VMEM_LIMIT_MAX)))


def _tile(m, pref):
    t = min(m, pref)
    assert m % t == 0, (m, pref)
    return t


def _sigmoid(x):
    return 1.0 / (1.0 + jnp.exp(-x))


def _softplus(x):
    return jnp.maximum(x, 0.0) + jnp.log1p(jnp.exp(-jnp.abs(x)))


def _split3(x):
    a = x.astype(BF16)
    r = x - a.astype(F32)
    b = r.astype(BF16)
    c = (r - b.astype(F32)).astype(BF16)
    return a, b, c


def _inproj_body(x_ref, g_ref, w_ref, wdt_ref, o_ref, dt_ref, hn_ref):
    @pl.when(pl.program_id(1) == 0)
    def _():
        x = x_ref[...]
        ms = jnp.mean(x * x, axis=-1, keepdims=True)
        hn_ref[...] = ((x * lax.rsqrt(ms + EPS)) * g_ref[...]).astype(BF16)
        dt_ref[...] = jnp.dot(hn_ref[...], wdt_ref[...], preferred_element_type=F32)

    o_ref[...] = jnp.dot(hn_ref[...], w_ref[...], preferred_element_type=F32)


def _in_proj(x, g, w, wdt):
    m, d = x.shape
    n = w.shape[1]
    tm, tn = _tile(m, 1024), _tile(n, 1024)
    return pl.pallas_call(
        _inproj_body,
        grid=(m // tm, n // tn),
        in_specs=[pl.BlockSpec((tm, d), lambda i, j: (i, 0)),
                  pl.BlockSpec((1, d), lambda i, j: (0, 0)),
                  pl.BlockSpec((d, tn), lambda i, j: (0, j)),
                  pl.BlockSpec((d, LANES), lambda i, j: (0, 0))],
        out_specs=[pl.BlockSpec((tm, tn), lambda i, j: (i, j)),
                   pl.BlockSpec((tm, LANES), lambda i, j: (i, 0))],
        out_shape=[SDS((m, n), F32), SDS((m, LANES), F32)],
        scratch_shapes=[pltpu.VMEM((tm, d), BF16)],
        compiler_params=_cparams(("parallel", "arbitrary"),
                                 [((tm, d), F32, 3), ((tm, d), BF16, 1), ((d, tn), BF16, 2),
                                  ((tm, tn), F32, 3), ((tm, LANES), F32, 2), ((d, LANES), BF16, 2)]),
        name="in_proj",
    )(x, g, w, wdt)


_NT = (((1,), (1,)), ((), ()))
LOG2E = 1.4426950408889634


def _sb_logs(z):
    neg_abs = lax.bitcast_convert_type(lax.bitcast_convert_type(z, jnp.uint32) | jnp.uint32(0x80000000), F32)
    lbeta = jnp.minimum(z, 0.0) - jnp.log2(1.0 + jnp.exp2(neg_abs))
    return lbeta, lbeta - z


def _sb_update(qm, kblk, vblk, bias, uo, carry, acc, mask):
    z = lax.dot_general(qm, kblk, _NT, preferred_element_type=F32) + bias
    lbeta, lk = _sb_logs(z)
    lk = jnp.where(mask, lk, 0.0)
    r = jnp.dot(lk.astype(BF16), uo, preferred_element_type=F32)
    w = jnp.where(mask, jnp.exp2(lbeta + r[:, :TK_ATT] + carry), 0.0)
    acc = acc + jnp.dot(w.astype(BF16), vblk, preferred_element_type=F32)
    return carry + r[:, TK_ATT:], acc


def _add_from_row(x, r0, delta):
    if r0 == 0:
        return x + delta
    return jnp.concatenate([x[:r0, :], x[r0:, :] + delta], axis=0)


def _cumsum_matrix():
    jp = lax.broadcasted_iota(jnp.int32, (TK_ATT, 2 * TK_ATT), 0)
    j = lax.broadcasted_iota(jnp.int32, (TK_ATT, 2 * TK_ATT), 1)
    return jnp.where((jp > j) | (j >= TK_ATT), 1.0, 0.0).astype(BF16)


def _att_prompt_body(bias_ref, q_ref, k_ref, v_ref, g_ref, uo_ref, o_ref, kt_ref, vt_ref, carry_ref, acc_ref,
                     *, tq, scale):
    hp = pl.program_id(1)
    qi = pl.program_id(2)

    @pl.when(qi == 0)
    def _():
        for blk in range(k_ref.shape[0] // LANES):
            rows = slice(blk * LANES, (blk + 1) * LANES)
            kt_ref[0, :, rows] = k_ref[rows, :].T
            vt_ref[0, :, rows] = v_ref[rows, :].T

    nsub = tq // TK_ATT
    lane = lax.broadcasted_iota(jnp.int32, (1, LANES), 1)
    low = lane < D_HEAD_ATT
    uo = uo_ref[...]
    qb = (q_ref[...] * (scale * LOG2E)).astype(BF16)
    lane2 = lax.broadcasted_iota(jnp.int32, (1, 2 * TK_ATT), 1)
    bias = jnp.where(lane2 < TK_ATT, bias_ref[2 * hp], bias_ref[2 * hp + 1]) * LOG2E
    qrow = lax.broadcasted_iota(jnp.int32, (tq, 1), 0)
    carry_ref[...] = jnp.zeros_like(carry_ref)
    acc_ref[...] = jnp.zeros_like(acc_ref)

    def span_step(ks, nblk, masked):
        k0 = pl.multiple_of(ks * tq, tq)
        carries = [carry_ref[0], carry_ref[1]]
        accs = [acc_ref[0], acc_ref[1]]
        order = list(reversed(range(nblk)))
        zs, vblks, masks, row0s = [], [], [], []
        for sb in order:
            rows = pl.ds(k0 + sb * TK_ATT, TK_ATT)
            r0 = sb * TK_ATT if masked else 0
            kf = k_ref[rows, :]
            kbd = jnp.concatenate([jnp.where(low, kf, 0.0), jnp.where(low, 0.0, kf)], axis=0).astype(BF16)
            vblks.append(v_ref[rows, :].astype(BF16))
            zs.append(lax.dot_general(qb[r0:, :], kbd, _NT, preferred_element_type=F32) + bias)
            masks.append((sb * TK_ATT + (lane2 & (TK_ATT - 1)) < qrow[r0:, :]) if masked else None)
            row0s.append(r0)
        lbetas, rs = [], []
        for z, mask in zip(zs, masks):
            lbeta, lk = _sb_logs(z)
            if masked:
                lk = jnp.where(mask, lk, 0.0)
            lkb = lk.astype(BF16)
            lbetas.append(lbeta)
            rs.append([jnp.dot(lkb[:, hh * TK_ATT:(hh + 1) * TK_ATT], uo, preferred_element_type=F32)
                       for hh in range(2)])
        for lbeta, r2, mask, vblk, r0 in zip(lbetas, rs, masks, vblks, row0s):
            for hh in range(2):
                hs = slice(hh * TK_ATT, (hh + 1) * TK_ATT)
                w = jnp.exp2(lbeta[:, hs] + r2[hh][:, :TK_ATT] + carries[hh][r0:, :])
                if masked:
                    w = jnp.where(mask[:, hs], w, 0.0)
                pv = jnp.dot(w.astype(BF16), vblk, preferred_element_type=F32)
                accs[hh] = _add_from_row(accs[hh], r0, pv)
                carries[hh] = _add_from_row(carries[hh], r0, r2[hh][:, TK_ATT:])
        for hh in range(2):
            carry_ref[hh] = carries[hh]
            acc_ref[hh] = accs[hh]

    span_step(qi, nsub, True)

    def off_diag(t, _):
        span_step(qi - 1 - t, nsub, False)
        return 0

    lax.fori_loop(0, qi, off_diag, 0)
    o = jnp.where(low, acc_ref[0], acc_ref[1])
    sq = o * o
    s0 = jnp.sum(jnp.where(low, sq, 0.0), axis=-1, keepdims=True)
    s1 = jnp.sum(jnp.where(low, 0.0, sq), axis=-1, keepdims=True)
    ms = jnp.where(low, s0, s1) * (1.0 / D_HEAD_ATT)
    o_ref[...] = ((o * lax.rsqrt(ms + EPS)) * g_ref[...]).astype(o_ref.dtype)


def _att_prompt(proj, bias, g, uo, nb, seq):
    m = proj.shape[0]
    tq = _tile(seq, 512)
    nq = seq // tq
    npair = H_ATT // 2
    body = functools.partial(_att_prompt_body, tq=tq, scale=D_HEAD_ATT ** -0.5)
    return pl.pallas_call(
        body,
        grid=(nb, npair, nq),
        in_specs=[pl.BlockSpec(memory_space=pltpu.SMEM),
                  pl.BlockSpec((tq, LANES), lambda b, h, i: (b * nq + i, COL_Q // LANES + h)),
                  pl.BlockSpec((seq, LANES), lambda b, h, i: (b, COL_K // LANES + h)),
                  pl.BlockSpec((seq, LANES), lambda b, h, i: (b, COL_V // LANES + h)),
                  pl.BlockSpec((1, LANES), lambda b, h, i: (0, h)),
                  pl.BlockSpec((TK_ATT, 2 * TK_ATT), lambda b, h, i: (0, 0))],
        out_specs=[pl.BlockSpec((tq, LANES), lambda b, h, i: (b * nq + i, h)),
                   pl.BlockSpec((1, LANES, seq), lambda b, h, i: (b, h, 0)),
                   pl.BlockSpec((1, LANES, seq), lambda b, h, i: (b, h, 0))],
        out_shape=[SDS((m, D_ATT), BF16), SDS((nb, D_ATT, seq), F32), SDS((nb, D_ATT, seq), F32)],
        scratch_shapes=[pltpu.VMEM((2, tq, TK_ATT), F32), pltpu.VMEM((2, tq, LANES), F32)],
        compiler_params=_cparams(("parallel", "parallel", "arbitrary"),
                                 [((seq, LANES), F32, 8), ((tq, LANES), F32, 24)]),
        name="att_prompt",
    )(bias, proj, proj, proj, g, uo)


def _att_sample_body(pt_ref, q_ref, kn_ref, vn_ref, bias_ref, g_ref, uo_ref, *rest, npg, scale):
    k_refs, v_refs = rest[:npg], rest[npg:2 * npg]
    o_ref, qbd_ref, kpad_ref, vpad_ref, carry_ref, acc_ref = rest[2 * npg:]
    del pt_ref
    p = pl.program_id(1)
    nrow = H_ATT * SUBLANES
    row = lax.broadcasted_iota(jnp.int32, (nrow, D_ATT), 0)
    lane = lax.broadcasted_iota(jnp.int32, (nrow, D_ATT), 1)
    own = (row >> 3) == (lane >> 6)
    uo = uo_ref[...]
    bias = bias_ref[...] * LOG2E

    @pl.when(p == 0)
    def _():
        q = q_ref[...] * (scale * LOG2E)
        qt = jnp.concatenate([q] * H_ATT, axis=0)
        qbd = jnp.where(own, qt, 0.0).astype(BF16)
        qbd_ref[...] = qbd
        kpad_ref[...] = jnp.zeros_like(kpad_ref)
        vpad_ref[...] = jnp.zeros_like(vpad_ref)
        kpad_ref[0:SUBLANES, :] = kn_ref[...]
        vpad_ref[0:SUBLANES, :] = vn_ref[...]
        tok = lax.broadcasted_iota(jnp.int32, (nrow, 1), 0) & (SUBLANES - 1)
        mask = lax.broadcasted_iota(jnp.int32, (1, TK_ATT), 1) < tok
        carry, acc = _sb_update(qbd, kpad_ref[...].astype(BF16), vpad_ref[...].astype(BF16), bias, uo,
                                jnp.zeros((nrow, TK_ATT), F32), jnp.zeros((nrow, D_ATT), F32), mask)
        carry_ref[...] = carry
        acc_ref[...] = acc

    carry, acc = carry_ref[...], acc_ref[...]
    qbd = qbd_ref[...]
    zs = [jnp.dot(qbd, k_refs[gi][0].astype(BF16), preferred_element_type=F32) + bias for gi in range(npg)]
    lbetas, rs = [], []
    for z in zs:
        lbeta, lk = _sb_logs(z)
        lbetas.append(lbeta)
        rs.append(jnp.dot(lk.astype(BF16), uo, preferred_element_type=F32))
    for gi in range(npg):
        w = jnp.exp2(lbetas[gi] + rs[gi][:, :TK_ATT] + carry)
        acc = acc + lax.dot_general(w.astype(BF16), v_refs[gi][0].astype(BF16), _NT, preferred_element_type=F32)
        carry = carry + rs[gi][:, TK_ATT:]
    carry_ref[...] = carry
    acc_ref[...] = acc

    @pl.when(p == pl.num_programs(1) - 1)
    def _():
        a = jnp.where(own, acc, 0.0)
        ms = jnp.sum(a * a, axis=-1, keepdims=True) * (1.0 / D_HEAD_ATT)
        a = a * lax.rsqrt(ms + EPS)
        o = a[0:SUBLANES]
        for h in range(1, H_ATT):
            o = o + a[h * SUBLANES:(h + 1) * SUBLANES]
        o_ref[...] = o * g_ref[...]


def _att_sample(proj, cache_k, cache_v, page_table, layer, n_pool, bias_rows, g, uo):
    m = proj.shape[0]
    nseq, npages = page_table.shape
    assert m == nseq * SUBLANES
    npg = next(n for n in (8, 4, 2, 1) if npages % n == 0)
    pt = page_table.reshape(-1)
    ck = jnp.transpose(cache_k, (0, 1, 3, 4, 2)).reshape(-1, D_ATT, PAGE_SIZE)
    cv = jnp.transpose(cache_v, (0, 1, 3, 4, 2)).reshape(-1, D_ATT, PAGE_SIZE)
    nrow = H_ATT * SUBLANES

    def page_map(gi):
        def f(b, p, pt_ref):
            return (layer * n_pool + pt_ref[b * npages + (npages - 1 - (p * npg + gi))], 0, 0)
        return f

    page_specs = [pl.BlockSpec((1, D_ATT, PAGE_SIZE), page_map(gi)) for gi in range(npg)]
    body = functools.partial(_att_sample_body, npg=npg, scale=D_HEAD_ATT ** -0.5)
    grid_spec = pltpu.PrefetchScalarGridSpec(
        num_scalar_prefetch=1,
        grid=(nseq, npages // npg),
        in_specs=[pl.BlockSpec((SUBLANES, D_ATT), lambda b, p, pt_ref: (b, COL_Q // D_ATT)),
                  pl.BlockSpec((SUBLANES, D_ATT), lambda b, p, pt_ref: (b, COL_K // D_ATT)),
                  pl.BlockSpec((SUBLANES, D_ATT), lambda b, p, pt_ref: (b, COL_V // D_ATT)),
                  pl.BlockSpec((nrow, TK_ATT), lambda b, p, pt_ref: (0, 0)),
                  pl.BlockSpec((1, D_ATT), lambda b, p, pt_ref: (0, 0)),
                  pl.BlockSpec((TK_ATT, 2 * TK_ATT), lambda b, p, pt_ref: (0, 0))] + page_specs + page_specs,
        out_specs=pl.BlockSpec((SUBLANES, D_ATT), lambda b, p, pt_ref: (b, 0)),
        scratch_shapes=[pltpu.VMEM((nrow, D_ATT), BF16),
                        pltpu.VMEM((TK_ATT, D_ATT), F32),
                        pltpu.VMEM((TK_ATT, D_ATT), F32),
                        pltpu.VMEM((nrow, TK_ATT), F32),
                        pltpu.VMEM((nrow, D_ATT), F32)])
    return pl.pallas_call(
        body,
        grid_spec=grid_spec,
        out_shape=SDS((m, D_ATT), F32),
        compiler_params=_cparams(("parallel", "arbitrary"),
                                 [((PAGE_SIZE, D_ATT), F32, 4 * npg), ((TK_ATT, D_ATT), F32, 4)]),
        name="att_sample",
    )(pt, proj, proj, proj, bias_rows, g, uo, *([ck] * npg), *([cv] * npg))


def _ssd_body(*refs, lr, nc, has_h0):
    if has_h0:
        (z_ref, xs_ref, bc_ref, dt_ref, c0_ref, h0_ref, cw_ref, cb_ref, dtb_ref, alog_ref, dd_ref, ng_ref,
         tri_ref, y_ref, ht_ref, win_ref, hs_ref, dts_ref, ysc_ref, xd_ref, et_ref) = refs
    else:
        (z_ref, xs_ref, bc_ref, dt_ref, c0_ref, cw_ref, cb_ref, dtb_ref, alog_ref, dd_ref, ng_ref,
         tri_ref, y_ref, ht_ref, win_ref, hs_ref, dts_ref, ysc_ref, xd_ref, et_ref) = refs
        h0_ref = None
    L = SSD_CHUNK
    c = pl.program_id(1)
    hist = SUBLANES
    pairs_per_group = H_SSD // G_SSD // 2

    @pl.when(c == 0)
    def _():
        win_ref[0:hist, :] = c0_ref[0]
        if lr < L:
            win_ref[hist:hist + L, :] = jnp.zeros((L, SSD_CONV_DIM), F32)
            dts_ref[...] = jnp.zeros_like(dts_ref)
        if has_h0:
            for hp in range(H_SSD // 2):
                pair = jnp.concatenate([h0_ref[0, 2 * hp], h0_ref[0, 2 * hp + 1]], axis=0)
                gsl = slice((hp % pairs_per_group) * LANES, (hp % pairs_per_group + 1) * LANES)
                hs_ref[hp // pairs_per_group, :, gsl] = pair.T
        else:
            hs_ref[...] = jnp.zeros_like(hs_ref)

    win_ref[hist:hist + lr, 0:D_SSD] = xs_ref[...]
    win_ref[hist:hist + lr, D_SSD:SSD_CONV_DIM] = bc_ref[...]
    dts_ref[0:lr, :] = dt_ref[...]

    lane = lax.broadcasted_iota(jnp.int32, (1, LANES), 1)
    low = lane < P_SSD
    rowi = lax.broadcasted_iota(jnp.int32, (L, 1), 0)
    dt = _softplus(dts_ref[...] + dtb_ref[...])
    if lr < L:
        dt = jnp.where(rowi < lr, dt, 0.0)
    a_head = jnp.where(lane < H_SSD, -jnp.exp(alog_ref[...]), 0.0)
    a = dt * a_head
    tri = tri_ref[...]
    a1, a2, a3 = _split3(a)
    a_cs = (jnp.dot(tri, a1, preferred_element_type=F32) + jnp.dot(tri, a2, preferred_element_type=F32)
            + jnp.dot(tri, a3, preferred_element_type=F32))
    a_cs_t = a_cs.T
    a_tot = a_cs[L - 1:L, :]
    causal = rowi >= lax.broadcasted_iota(jnp.int32, (1, L), 1)

    pre = cb_ref[...]
    for k in range(SSD_CONV_W):
        off = hist - (SSD_CONV_W - 1) + k
        pre = pre + cw_ref[k:k + 1, :] * win_ref[off:off + L, :]
    xbc = pre * _sigmoid(pre)
    win_ref[0:hist, :] = win_ref[L:L + hist, :]

    cbs, bts, cgs = [], [], []
    for g in range(G_SSD):
        bg = xbc[:, D_SSD + g * N_SSD:D_SSD + (g + 1) * N_SSD]
        cg = xbc[:, D_SSD + (G_SSD + g) * N_SSD:D_SSD + (G_SSD + g + 1) * N_SSD].astype(BF16)
        cbs.append(lax.dot_general(cg, bg.astype(BF16), (((1,), (1,)), ((), ())), preferred_element_type=F32))
        bts.append(bg.T.astype(BF16))
        cgs.append(cg)

    for hp in range(H_SSD // 2):
        g = hp // pairs_per_group
        cols, dtcols, mats = [], [], []
        for h in (2 * hp, 2 * hp + 1):
            col = jnp.broadcast_to(a_cs[:, h:h + 1], (L, LANES))
            dec = jnp.exp(jnp.where(causal, col - a_cs_t[h:h + 1, :], -1e30))
            mats.append((cbs[g] * dec).astype(BF16))
            cols.append(col)
            dtcols.append(jnp.broadcast_to(dt[:, h:h + 1], (L, LANES)))
        sl = slice(hp * LANES, (hp + 1) * LANES)
        xs_pair = xbc[:, sl]
        xdt = xs_pair * jnp.where(low, dtcols[0], dtcols[1])
        xdt_bf = xdt.astype(BF16)
        ydiag = jnp.where(low, jnp.dot(mats[0], xdt_bf, preferred_element_type=F32),
                          jnp.dot(mats[1], xdt_bf, preferred_element_type=F32))
        cs_pair = jnp.where(low, cols[0], cols[1])
        gsl = slice((hp % pairs_per_group) * LANES, (hp % pairs_per_group + 1) * LANES)
        yoff = jnp.dot(cgs[g], hs_ref[g, :, gsl].astype(BF16), preferred_element_type=F32) * jnp.exp(cs_pair)
        ysc_ref[:, sl] = ydiag + yoff + dd_ref[:, sl] * xs_pair
        tot_pair = jnp.where(low, jnp.broadcast_to(a_tot[:, 2 * hp:2 * hp + 1], (1, LANES)),
                             jnp.broadcast_to(a_tot[:, 2 * hp + 1:2 * hp + 2], (1, LANES)))
        xd_ref[:, sl] = (xdt * jnp.exp(tot_pair - cs_pair)).astype(BF16)
        et_ref[:, sl] = jnp.exp(tot_pair)

    gw = D_SSD // G_SSD
    for g in range(G_SSD):
        gs = slice(g * gw, (g + 1) * gw)
        cst = jnp.dot(bts[g], xd_ref[:, gs], preferred_element_type=F32)
        hs_ref[g] = hs_ref[g] * et_ref[:, gs] + cst

    zz = z_ref[...]
    y = ysc_ref[0:lr, :] * (zz * _sigmoid(zz))
    for g in range(G_SSD):
        gs = slice(g * gw, (g + 1) * gw)
        yg = y[:, gs]
        ms = jnp.mean(yg * yg, axis=-1, keepdims=True)
        y_ref[:, gs] = ((yg * lax.rsqrt(ms + EPS)) * ng_ref[:, gs]).astype(y_ref.dtype)

    @pl.when(c == nc - 1)
    def _():
        for hp in range(H_SSD // 2):
            gsl = slice((hp % pairs_per_group) * LANES, (hp % pairs_per_group + 1) * LANES)
            pair_t = hs_ref[hp // pairs_per_group, :, gsl].T
            ht_ref[0, 2 * hp] = pair_t[0:P_SSD]
            ht_ref[0, 2 * hp + 1] = pair_t[P_SSD:2 * P_SSD]


def _ssd(proj, dtp, conv0, h0, cw, cb, dtb, alog, dd, ng, tri, nb, seq, out_dtype):
    m = proj.shape[0]
    lr = min(seq, SSD_CHUNK)
    assert seq % lr == 0
    nc = seq // lr
    has_h0 = h0 is not None
    gw = D_SSD // G_SSD
    row = lambda b, c: b * nc + c
    in_specs = [pl.BlockSpec((lr, D_SSD), lambda b, c: (row(b, c), COL_Z // D_SSD)),
                pl.BlockSpec((lr, D_SSD), lambda b, c: (row(b, c), COL_XS // D_SSD)),
                pl.BlockSpec((lr, 2 * G_SSD * N_SSD), lambda b, c: (row(b, c), COL_BC // (2 * G_SSD * N_SSD))),
                pl.BlockSpec((lr, LANES), lambda b, c: (row(b, c), 0)),
                pl.BlockSpec((1, SUBLANES, SSD_CONV_DIM), lambda b, c: (b, 0, 0))]
    args = [proj, proj, proj, dtp, conv0]
    if has_h0:
        in_specs.append(pl.BlockSpec((1, H_SSD, P_SSD, N_SSD), lambda b, c: (b, 0, 0, 0)))
        args.append(h0)
    const = lambda shape: pl.BlockSpec(shape, lambda b, c: (0,) * len(shape))
    in_specs += [const((SSD_CONV_W, SSD_CONV_DIM)), const((1, SSD_CONV_DIM)), const((1, LANES)),
                 const((1, LANES)), const((1, D_SSD)), const((1, D_SSD)), const((SSD_CHUNK, SSD_CHUNK))]
    args += [cw, cb, dtb, alog, dd, ng, tri]
    body = functools.partial(_ssd_body, lr=lr, nc=nc, has_h0=has_h0)
    L = SSD_CHUNK
    return pl.pallas_call(
        body,
        grid=(nb, nc),
        in_specs=in_specs,
        out_specs=[pl.BlockSpec((lr, D_SSD), lambda b, c: (row(b, c), 0)),
                   pl.BlockSpec((1, H_SSD, P_SSD, N_SSD), lambda b, c: (b, 0, 0, 0))],
        out_shape=[SDS((m, D_SSD), out_dtype), SDS((nb, H_SSD, P_SSD, N_SSD), F32)],
        scratch_shapes=[pltpu.VMEM((L + SUBLANES, SSD_CONV_DIM), F32),
                        pltpu.VMEM((G_SSD, N_SSD, gw), F32),
                        pltpu.VMEM((L, LANES), F32),
                        pltpu.VMEM((L, D_SSD), F32),
                        pltpu.VMEM((L, D_SSD), BF16),
                        pltpu.VMEM((1, D_SSD), F32)],
        compiler_params=_cparams(("parallel", "arbitrary"),
                                 [((L, SSD_CONV_DIM), F32, 12), ((G_SSD, N_SSD, gw), F32, 5)]),
        name="ssd",
    )(*args)


def _conf_body(glu_ref, c0_ref, w_ref, b_ref, lg_ref, lb_ref, o_ref, tail_ref, win_ref, part_ref, *, tt, nt):
    t = pl.program_id(1)

    @pl.when(t == 0)
    def _():
        win_ref[0:CONF_PAD, :] = c0_ref[0]

    glu = glu_ref[...]
    win_ref[CONF_PAD:CONF_PAD + tt, :] = glu[:, :C_CONF] * _sigmoid(glu[:, C_CONF:])
    first = CONF_PAD - (CONF_W - 1)
    acc = jnp.broadcast_to(b_ref[...], (tt, C_CONF))
    for r in range(SUBLANES):
        offs = [j for j in range(r, first + CONF_W, SUBLANES) if j >= first]
        rows = tt if r == 0 else tt + SUBLANES
        part = w_ref[offs[0] - first:offs[0] - first + 1, :] * win_ref[offs[0] - r:offs[0] - r + rows, :]
        for j in offs[1:]:
            part = part + w_ref[j - first:j - first + 1, :] * win_ref[j - r:j - r + rows, :]
        if r == 0:
            acc = acc + part
        else:
            part_ref[r - 1] = part
            acc = acc + part_ref[r - 1, r:r + tt, :]
    mu = jnp.mean(acc, axis=-1, keepdims=True)
    xc = acc - mu
    var = jnp.mean(xc * xc, axis=-1, keepdims=True)
    yn = (xc * lax.rsqrt(var + EPS)) * lg_ref[...] + lb_ref[...]
    o_ref[...] = (yn * _sigmoid(yn)).astype(o_ref.dtype)
    tail = win_ref[tt:tt + CONF_PAD, :]
    tail_ref[0] = tail
    if nt > 1:
        win_ref[0:CONF_PAD, :] = tail


def _conformer(proj, conv0, w, b, lg, lb, nb, seq, out_dtype):
    m = proj.shape[0]
    tt = _tile(seq, 256)
    nt = seq // tt
    assert nt == 1 or tt >= CONF_PAD
    body = functools.partial(_conf_body, tt=tt, nt=nt)
    const = lambda shape: pl.BlockSpec(shape, lambda bb, t: (0,) * len(shape))
    return pl.pallas_call(
        body,
        grid=(nb, nt),
        in_specs=[pl.BlockSpec((tt, 2 * C_CONF), lambda bb, t: (bb * nt + t, COL_GLU // (2 * C_CONF))),
                  pl.BlockSpec((1, CONF_PAD, C_CONF), lambda bb, t: (bb, 0, 0)),
                  const((CONF_W, C_CONF)), const((1, C_CONF)), const((1, C_CONF)), const((1, C_CONF))],
        out_specs=[pl.BlockSpec((tt, C_CONF), lambda bb, t: (bb * nt + t, 0)),
                   pl.BlockSpec((1, CONF_PAD, C_CONF), lambda bb, t: (bb, 0, 0))],
        out_shape=[SDS((m, C_CONF), out_dtype), SDS((nb, CONF_PAD, C_CONF), F32)],
        scratch_shapes=[pltpu.VMEM((CONF_PAD + tt, C_CONF), F32),
                        pltpu.VMEM((SUBLANES - 1, tt + SUBLANES, C_CONF), F32)],
        compiler_params=_cparams(("parallel", "arbitrary"),
                                 [((tt, 2 * C_CONF), F32, 8), ((SUBLANES, tt + SUBLANES, C_CONF), F32, 1)]),
        name="conformer",
    )(proj, conv0, w, b, lg, lb)


def _outproj_body(x_ref, a_ref, y_ref, u_ref, w_ref, g_ref, x1_ref, h_ref):
    tm = x_ref.shape[0]
    halves = [slice(0, tm // 2), slice(tm // 2, tm)] if tm % (4 * SUBLANES) == 0 else [slice(0, tm)]
    prods = []
    for rs in halves:
        p = jnp.dot(a_ref[rs, :].astype(BF16), w_ref[0:D_ATT, :], preferred_element_type=F32)
        p = p + jnp.dot(y_ref[rs, :].astype(BF16), w_ref[D_ATT:D_ATT + D_SSD, :], preferred_element_type=F32)
        p = p + jnp.dot(u_ref[rs, :].astype(BF16), w_ref[D_ATT + D_SSD:, :], preferred_element_type=F32)
        prods.append(p)
    for rs, p in zip(halves, prods):
        acc = x_ref[rs, :] + p
        x1_ref[rs, :] = acc
        ms = jnp.mean(acc * acc, axis=-1, keepdims=True)
        h_ref[rs, :] = ((acc * lax.rsqrt(ms + EPS)) * g_ref[...]).astype(BF16)


def _out_proj(x, att, y, u, w, g):
    m, d = x.shape
    tm = _tile(m, 512)
    row = lambda width: pl.BlockSpec((tm, width), lambda i: (i, 0))
    return pl.pallas_call(
        _outproj_body,
        grid=(m // tm,),
        in_specs=[row(d), row(D_ATT), row(D_SSD), row(C_CONF),
                  pl.BlockSpec((d, d), lambda i: (0, 0)), pl.BlockSpec((1, d), lambda i: (0, 0))],
        out_specs=[row(d), row(d)],
        out_shape=[SDS((m, d), F32), SDS((m, d), BF16)],
        compiler_params=_cparams(("parallel",), [((tm, d), F32, 7), ((d, d), BF16, 2)]),
        name="out_proj",
    )(x, att, y, u, w, g)


HALO = 16


def _ffn_up_body(*refs, tm, seq, sample):
    if sample:
        h_ref, wg_ref, wv_ref, cw_ref, cb_ref, s1_ref, s2_ref, act_ref, tail_ref, lhs_ref, gs_ref = refs
    else:
        h_ref, halo_ref, wg_ref, wv_ref, cw_ref, cb_ref, act_ref, tail_ref, lhs_ref, gs_ref = refs
    i = pl.program_id(0)

    @pl.when(pl.program_id(1) == 0)
    def _():
        lhs_ref[HALO:, :] = h_ref[...]
        if sample:
            lhs_ref[0:HALO, :] = jnp.zeros((HALO, D_MODEL), BF16)
        else:
            lhs_ref[0:HALO, :] = halo_ref[...]

            @pl.when((i * tm) % seq == 0)
            def _():
                lhs_ref[0:HALO, :] = jnp.zeros((HALO, D_MODEL), BF16)

    gs_ref[...] = jnp.dot(lhs_ref[...], wg_ref[...], preferred_element_type=F32)
    val = jnp.dot(lhs_ref[HALO:, :], wv_ref[...], preferred_element_type=F32)
    g0 = gs_ref[HALO:HALO + tm, :]
    g1 = gs_ref[HALO - 1:HALO - 1 + tm, :]
    g2 = gs_ref[HALO - 2:HALO - 2 + tm, :]
    if sample:
        tok = lax.broadcasted_iota(jnp.int32, (tm, 1), 0) & (seq - 1)
        g1 = jnp.where(tok < 1, s1_ref[...], g1)
        g2 = jnp.where(tok < 2, s2_ref[...], g2)
        tail_ref[...] = g0
    else:
        tail_ref[...] = gs_ref[HALO + tm - SUBLANES:HALO + tm, :]
    pre = cw_ref[0:1, :] * g2 + cw_ref[1:2, :] * g1 + cw_ref[2:3, :] * g0 + cb_ref[...]
    act_ref[...] = ((pre * _sigmoid(pre)) * val).astype(BF16)


def _ffn_up(h, wg, wv, cw, cb, seq, s1=None, s2=None):
    m, d = h.shape
    f = wg.shape[1]
    sample = s1 is not None
    tm, tn = _tile(m, m if sample else min(1024, seq)), FFN_TN
    if sample:
        assert seq & (seq - 1) == 0 and seq >= FFN_CONV_W - 1
    else:
        assert seq % tm == 0
    body = functools.partial(_ffn_up_body, tm=tm, seq=seq, sample=sample)
    hspec = pl.BlockSpec((tm, d), lambda i, j: (i, 0))
    wspec = pl.BlockSpec((d, tn), lambda i, j: (0, j))
    cspec = lambda r: pl.BlockSpec((r, tn), lambda i, j: (0, j))
    tile = pl.BlockSpec((tm, tn), lambda i, j: (i, j))
    if sample:
        in_specs = [hspec, wspec, wspec, cspec(FFN_CONV_W), cspec(1), tile, tile]
        args = (h, wg, wv, cw, cb, s1, s2)
        tail_rows, tail_spec = m, tile
    else:
        halo = pl.BlockSpec((HALO, d), lambda i, j: (jnp.maximum(i * (tm // HALO) - 1, 0), 0))
        in_specs = [hspec, halo, wspec, wspec, cspec(FFN_CONV_W), cspec(1)]
        args = (h, h, wg, wv, cw, cb)
        tail_rows, tail_spec = (m // tm) * SUBLANES, pl.BlockSpec((SUBLANES, tn), lambda i, j: (i, j))
    return pl.pallas_call(
        body,
        grid=(m // tm, pl.cdiv(f, tn)),
        in_specs=in_specs,
        out_specs=[tile, tail_spec],
        out_shape=[SDS((m, f), BF16), SDS((tail_rows, f), F32)],
        scratch_shapes=[pltpu.VMEM((HALO + tm, d), BF16), pltpu.VMEM((HALO + tm, tn), F32)],
        compiler_params=_cparams(("parallel", "arbitrary"),
                                 [((tm, d), BF16, 3), ((d, tn), BF16, 4), ((tm, tn), F32, 8)]),
        name="ffn_up",
    )(*args)


def _ffn_down_body(x_ref, a_ref, w_ref, o_ref):
    o_ref[...] = x_ref[...] + jnp.dot(a_ref[...], w_ref[...], preferred_element_type=F32)


def _ffn_down(x, act, w):
    m, d = x.shape
    f = act.shape[1]
    tm, tn = _tile(m, 1024), _tile(d, 512)
    return pl.pallas_call(
        _ffn_down_body,
        grid=(m // tm, d // tn),
        in_specs=[pl.BlockSpec((tm, tn), lambda i, j: (i, j)),
                  pl.BlockSpec((tm, f), lambda i, j: (i, 0)),
                  pl.BlockSpec((f, tn), lambda i, j: (0, j))],
        out_specs=pl.BlockSpec((tm, tn), lambda i, j: (i, j)),
        out_shape=SDS((m, d), F32),
        compiler_params=_cparams(("parallel", "arbitrary"),
                                 [((tm, f), BF16, 2), ((f, tn), BF16, 2), ((tm, tn), F32, 6)]),
        name="ffn_down",
    )(x, act, w)


def _final_norm_body(x_ref, g_ref, o_ref):
    x = x_ref[...]
    ms = jnp.mean(x * x, axis=-1, keepdims=True)
    o_ref[...] = (x * lax.rsqrt(ms + EPS)) * g_ref[...]


def _final_norm(x, g):
    m, d = x.shape
    tm = _tile(m, 512)
    return pl.pallas_call(
        _final_norm_body,
        grid=(m // tm,),
        in_specs=[pl.BlockSpec((tm, d), lambda i: (i, 0)), pl.BlockSpec((1, d), lambda i: (0, 0))],
        out_specs=pl.BlockSpec((tm, d), lambda i: (i, 0)),
        out_shape=SDS((m, d), F32),
        compiler_params=_cparams(("parallel",), [((tm, d), F32, 6)]),
        name="final_norm",
    )(x, g)


def _pad_cols(x, width):
    return jnp.pad(x, ((0, 0), (0, width - x.shape[1])))


def _layer_weights(l, p):
    w_in = p["w_in"][l]
    c_z, c_x, c_dt = 3 * D_ATT, 3 * D_ATT + D_SSD, 3 * D_ATT + D_SSD + SSD_CONV_DIM
    c_glu = c_dt + H_SSD
    order = [w_in[:, c_z:c_z + D_SSD],
             w_in[:, c_x:c_x + D_SSD],
             w_in[:, c_glu:c_glu + 2 * C_CONF],
             w_in[:, c_x + D_SSD:c_dt],
             w_in[:, 0:3 * D_ATT]]
    w_up = p["w_up"][l]
    row = lambda v: v.reshape(1, -1)
    return dict(
        w_main=jnp.concatenate(order, axis=1).astype(BF16),
        w_dt=_pad_cols(w_in[:, c_dt:c_dt + H_SSD], LANES).astype(BF16),
        g_mix=row(p["norm_mix_g"][l]),
        att_bias=p["att_logit_bias"][l],
        att_g=row(p["att_norm_g"][l]),
        ssd_cw=p["ssd_conv_w"][l], ssd_cb=row(p["ssd_conv_b"][l]),
        dtb=_pad_cols(row(p["ssd_dt_bias"][l]), LANES), alog=_pad_cols(row(p["ssd_a_log"][l]), LANES),
        dd=row(jnp.repeat(p["ssd_d"][l], P_SSD)), ssd_ng=row(p["ssd_norm_g"][l]),
        conf_w=p["conf_conv_w"][l], conf_b=row(p["conf_conv_b"][l]),
        conf_lg=row(p["conf_ln_g"][l]), conf_lb=row(p["conf_ln_b"][l]),
        w_out=p["w_out"][l].astype(BF16), g_ffn=row(p["norm_ffn_g"][l]),
        w_gate=w_up[:, :D_FF].astype(BF16), w_val=w_up[:, D_FF:].astype(BF16),
        ffn_cw=p["ffn_conv_w"][l], ffn_cb=row(p["ffn_conv_b"][l]),
        w_down=p["w_down"][l].astype(BF16),
    )


def _xbc_cols(proj3):
    return jnp.concatenate([proj3[..., COL_XS:COL_XS + D_SSD], proj3[..., COL_BC:COL_BC + 2 * G_SSD * N_SSD]],
                           axis=-1)


def kernel(x_prompt, x_sample, cache_k, cache_v, state_ssm, state_ssd_conv, state_conf_conv, state_ffn_conv, page_table, norm_mix_g, w_in, att_logit_bias, att_norm_g, ssd_conv_w, ssd_conv_b, ssd_dt_bias, ssd_a_log, ssd_d, ssd_norm_g, conf_conv_w, conf_conv_b, conf_ln_g, conf_ln_b, w_out, norm_ffn_g, w_up, ffn_conv_w, ffn_conv_b, w_down, norm_final_g):
    params = dict(norm_mix_g=norm_mix_g, w_in=w_in, att_logit_bias=att_logit_bias, att_norm_g=att_norm_g,
                  ssd_conv_w=ssd_conv_w, ssd_conv_b=ssd_conv_b, ssd_dt_bias=ssd_dt_bias, ssd_a_log=ssd_a_log,
                  ssd_d=ssd_d, ssd_norm_g=ssd_norm_g, conf_conv_w=conf_conv_w, conf_conv_b=conf_conv_b,
                  conf_ln_g=conf_ln_g, conf_ln_b=conf_ln_b, w_out=w_out, norm_ffn_g=norm_ffn_g, w_up=w_up,
                  ffn_conv_w=ffn_conv_w, ffn_conv_b=ffn_conv_b, w_down=w_down)
    depth = w_in.shape[0]
    bp, sp, d = x_prompt.shape
    bs, ss, _ = x_sample.shape
    assert ss == SUBLANES and d == D_MODEL
    n_pool = cache_k.shape[1]
    xp = x_prompt.reshape(bp * sp, d)
    xs = x_sample.reshape(bs * ss, d)
    uo = _cumsum_matrix()
    tri = jnp.tril(jnp.ones((SSD_CHUNK, SSD_CHUNK), F32)).astype(BF16)
    zeros_ssd_conv = jnp.zeros((bp, SUBLANES, SSD_CONV_DIM), F32)
    zeros_conf_conv = jnp.zeros((bp, CONF_PAD, C_CONF), F32)
    outs_p, outs_s = [], []
    for l in range(depth):
        w = _layer_weights(l, params)

        proj, dtp = _in_proj(xp, w["g_mix"], w["w_main"], w["w_dt"])
        att, k_t, v_t = _att_prompt(proj, w["att_bias"], w["att_g"], uo, bp, sp)
        y, h_t = _ssd(proj, dtp, zeros_ssd_conv, None, w["ssd_cw"], w["ssd_cb"], w["dtb"], w["alog"], w["dd"],
                      w["ssd_ng"], tri, bp, sp, BF16)
        u, conf_tail = _conformer(proj, zeros_conf_conv, w["conf_w"], w["conf_b"], w["conf_lg"], w["conf_lb"],
                                  bp, sp, BF16)
        x1, h2 = _out_proj(xp, att, y, u, w["w_out"], w["g_ffn"])
        act, gate_tail = _ffn_up(h2, w["w_gate"], w["w_val"], w["ffn_cw"], w["ffn_cb"], sp)
        xp = _ffn_down(x1, act, w["w_down"])
        proj3 = proj.reshape(bp, sp, D_PROJ)
        tiles_per_seq = gate_tail.shape[0] // SUBLANES // bp
        gt = gate_tail.reshape(bp, tiles_per_seq, SUBLANES, D_FF)
        outs_p.append((
            k_t, v_t,
            h_t,
            _xbc_cols(proj3[:, sp - (SSD_CONV_W - 1):]),
            conf_tail[:, CONF_PAD - (CONF_W - 1):],
            gt[:, -1, SUBLANES - (FFN_CONV_W - 1):]))

        proj, dtp = _in_proj(xs, w["g_mix"], w["w_main"], w["w_dt"])
        bias_rows = jnp.broadcast_to(jnp.repeat(w["att_bias"], SUBLANES)[:, None], (H_ATT * SUBLANES, TK_ATT))
        att = _att_sample(proj, cache_k, cache_v, page_table, l, n_pool, bias_rows, w["att_g"], uo)
        conv0 = jnp.pad(state_ssd_conv[l], ((0, 0), (SUBLANES - (SSD_CONV_W - 1), 0), (0, 0)))
        y, h_t = _ssd(proj, dtp, conv0, state_ssm[l], w["ssd_cw"], w["ssd_cb"], w["dtb"],
                      w["alog"], w["dd"], w["ssd_ng"], tri, bs, ss, F32)
        conf0 = jnp.pad(state_conf_conv[l], ((0, 0), (CONF_PAD - (CONF_W - 1), 0), (0, 0)))
        u, conf_tail = _conformer(proj, conf0, w["conf_w"], w["conf_b"], w["conf_lg"], w["conf_lb"], bs, ss, F32)
        x1, h2 = _out_proj(xs, att, y, u, w["w_out"], w["g_ffn"])
        st = state_ffn_conv[l]
        s1 = jnp.concatenate([st[:, 1:2], jnp.zeros((bs, ss - 1, D_FF), F32)], axis=1).reshape(bs * ss, -1)
        s2 = jnp.concatenate([st, jnp.zeros((bs, ss - 2, D_FF), F32)], axis=1).reshape(bs * ss, -1)
        act, gate_full = _ffn_up(h2, w["w_gate"], w["w_val"], w["ffn_cw"], w["ffn_cb"], ss, s1, s2)
        xs = _ffn_down(x1, act, w["w_down"])
        proj3 = proj.reshape(bs, ss, D_PROJ)
        outs_s.append((
            proj3[..., COL_K:COL_K + D_ATT].reshape(bs, ss, H_ATT, D_HEAD_ATT),
            proj3[..., COL_V:COL_V + D_ATT].reshape(bs, ss, H_ATT, D_HEAD_ATT),
            h_t,
            _xbc_cols(proj3[:, ss - (SSD_CONV_W - 1):]),
            conf_tail[:, CONF_PAD - (CONF_W - 1):],
            gate_full.reshape(bs, ss, D_FF)[:, ss - (FFN_CONV_W - 1):]))

    g_final = norm_final_g.reshape(1, d)
    y_prompt = _final_norm(xp, g_final).reshape(bp, sp, d)
    y_sample = _final_norm(xs, g_final).reshape(bs, ss, d)
    stack = lambda outs, i: jnp.stack([o[i] for o in outs])

    def kv_prompt(i):
        t = stack(outs_p, i).reshape(depth, bp, H_ATT, D_HEAD_ATT, sp)
        return jnp.transpose(t, (0, 1, 4, 2, 3))

    return (y_prompt, y_sample,
            kv_prompt(0), kv_prompt(1), stack(outs_s, 0), stack(outs_s, 1),
            stack(outs_p, 2), stack(outs_s, 2), stack(outs_p, 3), stack(outs_s, 3),
            stack(outs_p, 4), stack(outs_s, 4), stack(outs_p, 5), stack(outs_s, 5))
```

```python
import functools
import math

import jax
import jax.numpy as jnp
from jax import lax
from jax.experimental import pallas as pl
from jax.experimental.pallas import tpu as pltpu

F32 = jnp.float32
BF16 = jnp.bfloat16
SDS = jax.ShapeDtypeStruct

D_MODEL = 2048
D_HEAD_ATT = 64
D_ATT = 512
H_ATT = 8
D_SSD = 1024
P_SSD = 64
H_SSD = 16
G_SSD = 2
N_SSD = 128
SSD_CONV_W = 4
SSD_CONV_DIM = D_SSD + 2 * G_SSD * N_SSD
SSD_CHUNK = 128
C_CONF = 512
CONF_W = 31
D_FF = 5504
FFN_CONV_W = 3
PAGE_SIZE = 128
EPS = 1e-6

LANES = 128
SUBLANES = 8
VMEM_BYTES_V7X = 64 * 1024 * 1024
VMEM_LIMIT_MAX = VMEM_BYTES_V7X - 8 * 1024 * 1024
VMEM_COMPILER_SCRATCH = 16 * 1024 * 1024

D_PROJ = 5120
COL_Z, COL_XS, COL_GLU, COL_BC, COL_Q, COL_K, COL_V = 0, 1024, 2048, 3072, 3584, 4096, 4608
FFN_TN = 512
CONF_PAD = 32
TK_ATT = 128


def _nbytes(shape, dtype):
    return math.prod(shape) * jnp.dtype(dtype).itemsize


def _cparams(sem, bufs):
    est = sum(_nbytes(s, d) * n for s, d, n in bufs)
    return pltpu.CompilerParams(dimension_semantics=sem,
                                vmem_limit_bytes=int(min(est + VMEM_COMPILER_SCRATCH, VMEM_LIMIT_MAX)))


def _tile(m, pref):
    t = min(m, pref)
    assert m % t == 0, (m, pref)
    return t


def _sigmoid(x):
    return 1.0 / (1.0 + jnp.exp(-x))


def _softplus(x):
    return jnp.maximum(x, 0.0) + jnp.log1p(jnp.exp(-jnp.abs(x)))


def _split3(x):
    a = x.astype(BF16)
    r = x - a.astype(F32)
    b = r.astype(BF16)
    c = (r - b.astype(F32)).astype(BF16)
    return a, b, c


def _inproj_body(x_ref, g_ref, w_ref, wdt_ref, o_ref, dt_ref, hn_ref):
    @pl.when(pl.program_id(1) == 0)
    def _():
        x = x_ref[...]
        ms = jnp.mean(x * x, axis=-1, keepdims=True)
        hn_ref[...] = ((x * lax.rsqrt(ms + EPS)) * g_ref[...]).astype(BF16)
        dt_ref[...] = jnp.dot(hn_ref[...], wdt_ref[...], preferred_element_type=F32)

    o_ref[...] = jnp.dot(hn_ref[...], w_ref[...], preferred_element_type=F32)


def _in_proj(x, g, w, wdt):
    m, d = x.shape
    n = w.shape[1]
    tm, tn = _tile(m, 1024), _tile(n, 1024)
    return pl.pallas_call(
        _inproj_body,
        grid=(m // tm, n // tn),
        in_specs=[pl.BlockSpec((tm, d), lambda i, j: (i, 0)),
                  pl.BlockSpec((1, d), lambda i, j: (0, 0)),
                  pl.BlockSpec((d, tn), lambda i, j: (0, j)),
                  pl.BlockSpec((d, LANES), lambda i, j: (0, 0))],
        out_specs=[pl.BlockSpec((tm, tn), lambda i, j: (i, j)),
                   pl.BlockSpec((tm, LANES), lambda i, j: (i, 0))],
        out_shape=[SDS((m, n), F32), SDS((m, LANES), F32)],
        scratch_shapes=[pltpu.VMEM((tm, d), BF16)],
        compiler_params=_cparams(("parallel", "arbitrary"),
                                 [((tm, d), F32, 3), ((tm, d), BF16, 1), ((d, tn), BF16, 2),
                                  ((tm, tn), F32, 3), ((tm, LANES), F32, 2), ((d, LANES), BF16, 2)]),
        name="in_proj",
    )(x, g, w, wdt)


_NT = (((1,), (1,)), ((), ()))
LOG2E = 1.4426950408889634


def _sb_logs(z):
    neg_abs = lax.bitcast_convert_type(lax.bitcast_convert_type(z, jnp.uint32) | jnp.uint32(0x80000000), F32)
    lbeta = jnp.minimum(z, 0.0) - jnp.log2(1.0 + jnp.exp2(neg_abs))
    return lbeta, lbeta - z


def _sb_update(qm, kblk, vblk, bias, uo, carry, acc, mask):
    z = lax.dot_general(qm, kblk, _NT, preferred_element_type=F32) + bias
    lbeta, lk = _sb_logs(z)
    lk = jnp.where(mask, lk, 0.0)
    r = jnp.dot(lk.astype(BF16), uo, preferred_element_type=F32)
    w = jnp.where(mask, jnp.exp2(lbeta + r[:, :TK_ATT] + carry), 0.0)
    acc = acc + jnp.dot(w.astype(BF16), vblk, preferred_element_type=F32)
    return carry + r[:, TK_ATT:], acc


def _add_from_row(x, r0, delta):
    if r0 == 0:
        return x + delta
    return jnp.concatenate([x[:r0, :], x[r0:, :] + delta], axis=0)


def _cumsum_matrix():
    jp = lax.broadcasted_iota(jnp.int32, (TK_ATT, 2 * TK_ATT), 0)
    j = lax.broadcasted_iota(jnp.int32, (TK_ATT, 2 * TK_ATT), 1)
    return jnp.where((jp > j) | (j >= TK_ATT), 1.0, 0.0).astype(BF16)


def _att_prompt_body(bias_ref, q_ref, k_ref, v_ref, g_ref, uo_ref, o_ref, kt_ref, vt_ref, carry_ref, acc_ref,
                     *, tq, scale):
    hp = pl.program_id(1)
    qi = pl.program_id(2)

    @pl.when(qi == 0)
    def _():
        for blk in range(k_ref.shape[0] // LANES):
            rows = slice(blk * LANES, (blk + 1) * LANES)
            kt_ref[0, :, rows] = k_ref[rows, :].T
            vt_ref[0, :, rows] = v_ref[rows, :].T

    nsub = tq // TK_ATT
    lane = lax.broadcasted_iota(jnp.int32, (1, LANES), 1)
    low = lane < D_HEAD_ATT
    uo = uo_ref[...]
    qb = (q_ref[...] * (scale * LOG2E)).astype(BF16)
    lane2 = lax.broadcasted_iota(jnp.int32, (1, 2 * TK_ATT), 1)
    bias = jnp.where(lane2 < TK_ATT, bias_ref[2 * hp], bias_ref[2 * hp + 1]) * LOG2E
    qrow = lax.broadcasted_iota(jnp.int32, (tq, 1), 0)
    carry_ref[...] = jnp.zeros_like(carry_ref)
    acc_ref[...] = jnp.zeros_like(acc_ref)

    def span_step(ks, nblk, masked):
        k0 = pl.multiple_of(ks * tq, tq)
        carries = [carry_ref[0], carry_ref[1]]
        accs = [acc_ref[0], acc_ref[1]]
        order = list(reversed(range(nblk)))
        zs, vblks, masks, row0s = [], [], [], []
        for sb in order:
            rows = pl.ds(k0 + sb * TK_ATT, TK_ATT)
            r0 = sb * TK_ATT if masked else 0
            kf = k_ref[rows, :]
            kbd = jnp.concatenate([jnp.where(low, kf, 0.0), jnp.where(low, 0.0, kf)], axis=0).astype(BF16)
            vblks.append(v_ref[rows, :].astype(BF16))
            zs.append(lax.dot_general(qb[r0:, :], kbd, _NT, preferred_element_type=F32) + bias)
            masks.append((sb * TK_ATT + (lane2 & (TK_ATT - 1)) < qrow[r0:, :]) if masked else None)
            row0s.append(r0)
        lbetas, rs = [], []
        for z, mask in zip(zs, masks):
            lbeta, lk = _sb_logs(z)
            if masked:
                lk = jnp.where(mask, lk, 0.0)
            lkb = lk.astype(BF16)
            lbetas.append(lbeta)
            rs.append([jnp.dot(lkb[:, hh * TK_ATT:(hh + 1) * TK_ATT], uo, preferred_element_type=F32)
                       for hh in range(2)])
        for lbeta, r2, mask, vblk, r0 in zip(lbetas, rs, masks, vblks, row0s):
            for hh in range(2):
                hs = slice(hh * TK_ATT, (hh + 1) * TK_ATT)
                w = jnp.exp2(lbeta[:, hs] + r2[hh][:, :TK_ATT] + carries[hh][r0:, :])
                if masked:
                    w = jnp.where(mask[:, hs], w, 0.0)
                pv = jnp.dot(w.astype(BF16), vblk, preferred_element_type=F32)
                accs[hh] = _add_from_row(accs[hh], r0, pv)
                carries[hh] = _add_from_row(carries[hh], r0, r2[hh][:, TK_ATT:])
        for hh in range(2):
            carry_ref[hh] = carries[hh]
            acc_ref[hh] = accs[hh]

    span_step(qi, nsub, True)

    def off_diag(t, _):
        span_step(qi - 1 - t, nsub, False)
        return 0

    lax.fori_loop(0, qi, off_diag, 0)
    o = jnp.where(low, acc_ref[0], acc_ref[1])
    sq = o * o
    s0 = jnp.sum(jnp.where(low, sq, 0.0), axis=-1, keepdims=True)
    s1 = jnp.sum(jnp.where(low, 0.0, sq), axis=-1, keepdims=True)
    ms = jnp.where(low, s0, s1) * (1.0 / D_HEAD_ATT)
    o_ref[...] = ((o * lax.rsqrt(ms + EPS)) * g_ref[...]).astype(o_ref.dtype)


def _att_prompt(proj, bias, g, uo, nb, seq):
    m = proj.shape[0]
    tq = _tile(seq, 512)
    nq = seq // tq
    npair = H_ATT // 2
    body = functools.partial(_att_prompt_body, tq=tq, scale=D_HEAD_ATT ** -0.5)
    return pl.pallas_call(
        body,
        grid=(nb, npair, nq),
        in_specs=[pl.BlockSpec(memory_space=pltpu.SMEM),
                  pl.BlockSpec((tq, LANES), lambda b, h, i: (b * nq + i, COL_Q // LANES + h)),
                  pl.BlockSpec((seq, LANES), lambda b, h, i: (b, COL_K // LANES + h)),
                  pl.BlockSpec((seq, LANES), lambda b, h, i: (b, COL_V // LANES + h)),
                  pl.BlockSpec((1, LANES), lambda b, h, i: (0, h)),
                  pl.BlockSpec((TK_ATT, 2 * TK_ATT), lambda b, h, i: (0, 0))],
        out_specs=[pl.BlockSpec((tq, LANES), lambda b, h, i: (b * nq + i, h)),
                   pl.BlockSpec((1, LANES, seq), lambda b, h, i: (b, h, 0)),
                   pl.BlockSpec((1, LANES, seq), lambda b, h, i: (b, h, 0))],
        out_shape=[SDS((m, D_ATT), BF16), SDS((nb, D_ATT, seq), F32), SDS((nb, D_ATT, seq), F32)],
        scratch_shapes=[pltpu.VMEM((2, tq, TK_ATT), F32), pltpu.VMEM((2, tq, LANES), F32)],
        compiler_params=_cparams(("parallel", "parallel", "arbitrary"),
                                 [((seq, LANES), F32, 8), ((tq, LANES), F32, 24)]),
        name="att_prompt",
    )(bias, proj, proj, proj, g, uo)


def _att_sample_body(pt_ref, q_ref, kn_ref, vn_ref, bias_ref, g_ref, uo_ref, *rest, npg, scale):
    k_refs, v_refs = rest[:npg], rest[npg:2 * npg]
    o_ref, qbd_ref, kpad_ref, vpad_ref, carry_ref, acc_ref = rest[2 * npg:]
    del pt_ref
    p = pl.program_id(1)
    nrow = H_ATT * SUBLANES
    row = lax.broadcasted_iota(jnp.int32, (nrow, D_ATT), 0)
    lane = lax.broadcasted_iota(jnp.int32, (nrow, D_ATT), 1)
    own = (row >> 3) == (lane >> 6)
    uo = uo_ref[...]
    bias = bias_ref[...] * LOG2E

    @pl.when(p == 0)
    def _():
        q = q_ref[...] * (scale * LOG2E)
        qt = jnp.concatenate([q] * H_ATT, axis=0)
        qbd = jnp.where(own, qt, 0.0).astype(BF16)
        qbd_ref[...] = qbd
        kpad_ref[...] = jnp.zeros_like(kpad_ref)
        vpad_ref[...] = jnp.zeros_like(vpad_ref)
        kpad_ref[0:SUBLANES, :] = kn_ref[...]
        vpad_ref[0:SUBLANES, :] = vn_ref[...]
        tok = lax.broadcasted_iota(jnp.int32, (nrow, 1), 0) & (SUBLANES - 1)
        mask = lax.broadcasted_iota(jnp.int32, (1, TK_ATT), 1) < tok
        carry, acc = _sb_update(qbd, kpad_ref[...].astype(BF16), vpad_ref[...].astype(BF16), bias, uo,
                                jnp.zeros((nrow, TK_ATT), F32), jnp.zeros((nrow, D_ATT), F32), mask)
        carry_ref[...] = carry
        acc_ref[...] = acc

    carry, acc = carry_ref[...], acc_ref[...]
    qbd = qbd_ref[...]
    zs = [jnp.dot(qbd, k_refs[gi][0].astype(BF16), preferred_element_type=F32) + bias for gi in range(npg)]
    lbetas, rs = [], []
    for z in zs:
        lbeta, lk = _sb_logs(z)
        lbetas.append(lbeta)
        rs.append(jnp.dot(lk.astype(BF16), uo, preferred_element_type=F32))
    for gi in range(npg):
        w = jnp.exp2(lbetas[gi] + rs[gi][:, :TK_ATT] + carry)
        acc = acc + lax.dot_general(w.astype(BF16), v_refs[gi][0].astype(BF16), _NT, preferred_element_type=F32)
        carry = carry + rs[gi][:, TK_ATT:]
    carry_ref[...] = carry
    acc_ref[...] = acc

    @pl.when(p == pl.num_programs(1) - 1)
    def _():
        a = jnp.where(own, acc, 0.0)
        ms = jnp.sum(a * a, axis=-1, keepdims=True) * (1.0 / D_HEAD_ATT)
        a = a * lax.rsqrt(ms + EPS)
        o = a[0:SUBLANES]
        for h in range(1, H_ATT):
            o = o + a[h * SUBLANES:(h + 1) * SUBLANES]
        o_ref[...] = o * g_ref[...]


def _att_sample(proj, cache_k, cache_v, page_table, layer, n_pool, bias_rows, g, uo):
    m = proj.shape[0]
    nseq, npages = page_table.shape
    assert m == nseq * SUBLANES
    npg = next(n for n in (16, 8, 4, 2, 1) if npages % n == 0)
    pt = page_table.reshape(-1)
    ck = jnp.transpose(cache_k, (0, 1, 3, 4, 2)).reshape(-1, D_ATT, PAGE_SIZE)
    cv = jnp.transpose(cache_v, (0, 1, 3, 4, 2)).reshape(-1, D_ATT, PAGE_SIZE)
    nrow = H_ATT * SUBLANES

    def page_map(gi):
        def f(b, p, pt_ref):
            return (layer * n_pool + pt_ref[b * npages + (npages - 1 - (p * npg + gi))], 0, 0)
        return f

    page_specs = [pl.BlockSpec((1, D_ATT, PAGE_SIZE), page_map(gi)) for gi in range(npg)]
    body = functools.partial(_att_sample_body, npg=npg, scale=D_HEAD_ATT ** -0.5)
    grid_spec = pltpu.PrefetchScalarGridSpec(
        num_scalar_prefetch=1,
        grid=(nseq, npages // npg),
        in_specs=[pl.BlockSpec((SUBLANES, D_ATT), lambda b, p, pt_ref: (b, COL_Q // D_ATT)),
                  pl.BlockSpec((SUBLANES, D_ATT), lambda b, p, pt_ref: (b, COL_K // D_ATT)),
                  pl.BlockSpec((SUBLANES, D_ATT), lambda b, p, pt_ref: (b, COL_V // D_ATT)),
                  pl.BlockSpec((nrow, TK_ATT), lambda b, p, pt_ref: (0, 0)),
                  pl.BlockSpec((1, D_ATT), lambda b, p, pt_ref: (0, 0)),
                  pl.BlockSpec((TK_ATT, 2 * TK_ATT), lambda b, p, pt_ref: (0, 0))] + page_specs + page_specs,
        out_specs=pl.BlockSpec((SUBLANES, D_ATT), lambda b, p, pt_ref: (b, 0)),
        scratch_shapes=[pltpu.VMEM((nrow, D_ATT), BF16),
                        pltpu.VMEM((TK_ATT, D_ATT), F32),
                        pltpu.VMEM((TK_ATT, D_ATT), F32),
                        pltpu.VMEM((nrow, TK_ATT), F32),
                        pltpu.VMEM((nrow, D_ATT), F32)])
    return pl.pallas_call(
        body,
        grid_spec=grid_spec,
        out_shape=SDS((m, D_ATT), F32),
        compiler_params=_cparams(("parallel", "arbitrary"),
                                 [((PAGE_SIZE, D_ATT), F32, 4 * npg), ((TK_ATT, D_ATT), F32, 4)]),
        name="att_sample",
    )(pt, proj, proj, proj, bias_rows, g, uo, *([ck] * npg), *([cv] * npg))


def _ssd_body(*refs, lr, nc, has_h0):
    if has_h0:
        (z_ref, xs_ref, bc_ref, dt_ref, c0_ref, h0_ref, cw_ref, cb_ref, dtb_ref, alog_ref, dd_ref, ng_ref,
         tri_ref, y_ref, ht_ref, win_ref, hs_ref, dts_ref, ysc_ref, xd_ref, et_ref) = refs
    else:
        (z_ref, xs_ref, bc_ref, dt_ref, c0_ref, cw_ref, cb_ref, dtb_ref, alog_ref, dd_ref, ng_ref,
         tri_ref, y_ref, ht_ref, win_ref, hs_ref, dts_ref, ysc_ref, xd_ref, et_ref) = refs
        h0_ref = None
    L = SSD_CHUNK
    c = pl.program_id(1)
    hist = SUBLANES
    pairs_per_group = H_SSD // G_SSD // 2

    @pl.when(c == 0)
    def _():
        win_ref[0:hist, :] = c0_ref[0]
        if lr < L:
            win_ref[hist:hist + L, :] = jnp.zeros((L, SSD_CONV_DIM), F32)
            dts_ref[...] = jnp.zeros_like(dts_ref)
        if has_h0:
            for hp in range(H_SSD // 2):
                pair = jnp.concatenate([h0_ref[0, 2 * hp], h0_ref[0, 2 * hp + 1]], axis=0)
                gsl = slice((hp % pairs_per_group) * LANES, (hp % pairs_per_group + 1) * LANES)
                hs_ref[hp // pairs_per_group, :, gsl] = pair.T
        else:
            hs_ref[...] = jnp.zeros_like(hs_ref)

    win_ref[hist:hist + lr, 0:D_SSD] = xs_ref[...]
    win_ref[hist:hist + lr, D_SSD:SSD_CONV_DIM] = bc_ref[...]
    dts_ref[0:lr, :] = dt_ref[...]

    lane = lax.broadcasted_iota(jnp.int32, (1, LANES), 1)
    low = lane < P_SSD
    rowi = lax.broadcasted_iota(jnp.int32, (L, 1), 0)
    dt = _softplus(dts_ref[...] + dtb_ref[...])
    if lr < L:
        dt = jnp.where(rowi < lr, dt, 0.0)
    a_head = jnp.where(lane < H_SSD, -jnp.exp(alog_ref[...]), 0.0)
    a = dt * a_head
    tri = tri_ref[...]
    a1, a2, a3 = _split3(a)
    a_cs = (jnp.dot(tri, a1, preferred_element_type=F32) + jnp.dot(tri, a2, preferred_element_type=F32)
            + jnp.dot(tri, a3, preferred_element_type=F32))
    a_cs_t = a_cs.T
    a_tot = a_cs[L - 1:L, :]
    causal = rowi >= lax.broadcasted_iota(jnp.int32, (1, L), 1)

    pre = cb_ref[...]
    for k in range(SSD_CONV_W):
        off = hist - (SSD_CONV_W - 1) + k
        pre = pre + cw_ref[k:k + 1, :] * win_ref[off:off + L, :]
    xbc = pre * _sigmoid(pre)
    win_ref[0:hist, :] = win_ref[L:L + hist, :]

    cbs, bts, cgs = [], [], []
    for g in range(G_SSD):
        bg = xbc[:, D_SSD + g * N_SSD:D_SSD + (g + 1) * N_SSD]
        cg = xbc[:, D_SSD + (G_SSD + g) * N_SSD:D_SSD + (G_SSD + g + 1) * N_SSD].astype(BF16)
        cbs.append(lax.dot_general(cg, bg.astype(BF16), (((1,), (1,)), ((), ())), preferred_element_type=F32))
        bts.append(bg.T.astype(BF16))
        cgs.append(cg)

    for hp in range(H_SSD // 2):
        g = hp // pairs_per_group
        cols, dtcols, mats = [], [], []
        for h in (2 * hp, 2 * hp + 1):
            col = jnp.broadcast_to(a_cs[:, h:h + 1], (L, LANES))
            dec = jnp.exp(jnp.where(causal, col - a_cs_t[h:h + 1, :], -1e30))
            mats.append((cbs[g] * dec).astype(BF16))
            cols.append(col)
            dtcols.append(jnp.broadcast_to(dt[:, h:h + 1], (L, LANES)))
        sl = slice(hp * LANES, (hp + 1) * LANES)
        xs_pair = xbc[:, sl]
        xdt = xs_pair * jnp.where(low, dtcols[0], dtcols[1])
        xdt_bf = xdt.astype(BF16)
        ydiag = jnp.where(low, jnp.dot(mats[0], xdt_bf, preferred_element_type=F32),
                          jnp.dot(mats[1], xdt_bf, preferred_element_type=F32))
        cs_pair = jnp.where(low, cols[0], cols[1])
        gsl = slice((hp % pairs_per_group) * LANES, (hp % pairs_per_group + 1) * LANES)
        yoff = jnp.dot(cgs[g], hs_ref[g, :, gsl].astype(BF16), preferred_element_type=F32) * jnp.exp(cs_pair)
        ysc_ref[:, sl] = ydiag + yoff + dd_ref[:, sl] * xs_pair
        tot_pair = jnp.where(low, jnp.broadcast_to(a_tot[:, 2 * hp:2 * hp + 1], (1, LANES)),
                             jnp.broadcast_to(a_tot[:, 2 * hp + 1:2 * hp + 2], (1, LANES)))
        xd_ref[:, sl] = (xdt * jnp.exp(tot_pair - cs_pair)).astype(BF16)
        et_ref[:, sl] = jnp.exp(tot_pair)

    gw = D_SSD // G_SSD
    for g in range(G_SSD):
        gs = slice(g * gw, (g + 1) * gw)
        cst = jnp.dot(bts[g], xd_ref[:, gs], preferred_element_type=F32)
        hs_ref[g] = hs_ref[g] * et_ref[:, gs] + cst

    zz = z_ref[...]
    y = ysc_ref[0:lr, :] * (zz * _sigmoid(zz))
    for g in range(G_SSD):
        gs = slice(g * gw, (g + 1) * gw)
        yg = y[:, gs]
        ms = jnp.mean(yg * yg, axis=-1, keepdims=True)
        y_ref[:, gs] = ((yg * lax.rsqrt(ms + EPS)) * ng_ref[:, gs]).astype(y_ref.dtype)

    @pl.when(c == nc - 1)
    def _():
        for hp in range(H_SSD // 2):
            gsl = slice((hp % pairs_per_group) * LANES, (hp % pairs_per_group + 1) * LANES)
            pair_t = hs_ref[hp // pairs_per_group, :, gsl].T
            ht_ref[0, 2 * hp] = pair_t[0:P_SSD]
            ht_ref[0, 2 * hp + 1] = pair_t[P_SSD:2 * P_SSD]


def _ssd(proj, dtp, conv0, h0, cw, cb, dtb, alog, dd, ng, tri, nb, seq, out_dtype):
    m = proj.shape[0]
    lr = min(seq, SSD_CHUNK)
    assert seq % lr == 0
    nc = seq // lr
    has_h0 = h0 is not None
    gw = D_SSD // G_SSD
    row = lambda b, c: b * nc + c
    in_specs = [pl.BlockSpec((lr, D_SSD), lambda b, c: (row(b, c), COL_Z // D_SSD)),
                pl.BlockSpec((lr, D_SSD), lambda b, c: (row(b, c), COL_XS // D_SSD)),
                pl.BlockSpec((lr, 2 * G_SSD * N_SSD), lambda b, c: (row(b, c), COL_BC // (2 * G_SSD * N_SSD))),
                pl.BlockSpec((lr, LANES), lambda b, c: (row(b, c), 0)),
                pl.BlockSpec((1, SUBLANES, SSD_CONV_DIM), lambda b, c: (b, 0, 0))]
    args = [proj, proj, proj, dtp, conv0]
    if has_h0:
        in_specs.append(pl.BlockSpec((1, H_SSD, P_SSD, N_SSD), lambda b, c: (b, 0, 0, 0)))
        args.append(h0)
    const = lambda shape: pl.BlockSpec(shape, lambda b, c: (0,) * len(shape))
    in_specs += [const((SSD_CONV_W, SSD_CONV_DIM)), const((1, SSD_CONV_DIM)), const((1, LANES)),
                 const((1, LANES)), const((1, D_SSD)), const((1, D_SSD)), const((SSD_CHUNK, SSD_CHUNK))]
    args += [cw, cb, dtb, alog, dd, ng, tri]
    body = functools.partial(_ssd_body, lr=lr, nc=nc, has_h0=has_h0)
    L = SSD_CHUNK
    return pl.pallas_call(
        body,
        grid=(nb, nc),
        in_specs=in_specs,
        out_specs=[pl.BlockSpec((lr, D_SSD), lambda b, c: (row(b, c), 0)),
                   pl.BlockSpec((1, H_SSD, P_SSD, N_SSD), lambda b, c: (b, 0, 0, 0))],
        out_shape=[SDS((m, D_SSD), out_dtype), SDS((nb, H_SSD, P_SSD, N_SSD), F32)],
        scratch_shapes=[pltpu.VMEM((L + SUBLANES, SSD_CONV_DIM), F32),
                        pltpu.VMEM((G_SSD, N_SSD, gw), F32),
                        pltpu.VMEM((L, LANES), F32),
                        pltpu.VMEM((L, D_SSD), F32),
                        pltpu.VMEM((L, D_SSD), BF16),
                        pltpu.VMEM((1, D_SSD), F32)],
        compiler_params=_cparams(("parallel", "arbitrary"),
                                 [((L, SSD_CONV_DIM), F32, 12), ((G_SSD, N_SSD, gw), F32, 5)]),
        name="ssd",
    )(*args)


def _conf_body(glu_ref, c0_ref, w_ref, b_ref, lg_ref, lb_ref, o_ref, tail_ref, win_ref, part_ref, *, tt, nt):
    t = pl.program_id(1)

    @pl.when(t == 0)
    def _():
        win_ref[0:CONF_PAD, :] = c0_ref[0]

    glu = glu_ref[...]
    win_ref[CONF_PAD:CONF_PAD + tt, :] = glu[:, :C_CONF] * _sigmoid(glu[:, C_CONF:])
    first = CONF_PAD - (CONF_W - 1)
    acc = jnp.broadcast_to(b_ref[...], (tt, C_CONF))
    for r in range(SUBLANES):
        offs = [j for j in range(r, first + CONF_W, SUBLANES) if j >= first]
        rows = tt if r == 0 else tt + SUBLANES
        part = w_ref[offs[0] - first:offs[0] - first + 1, :] * win_ref[offs[0] - r:offs[0] - r + rows, :]
        for j in offs[1:]:
            part = part + w_ref[j - first:j - first + 1, :] * win_ref[j - r:j - r + rows, :]
        if r == 0:
            acc = acc + part
        else:
            part_ref[r - 1] = part
            acc = acc + part_ref[r - 1, r:r + tt, :]
    mu = jnp.mean(acc, axis=-1, keepdims=True)
    xc = acc - mu
    var = jnp.mean(xc * xc, axis=-1, keepdims=True)
    yn = (xc * lax.rsqrt(var + EPS)) * lg_ref[...] + lb_ref[...]
    o_ref[...] = (yn * _sigmoid(yn)).astype(o_ref.dtype)
    tail = win_ref[tt:tt + CONF_PAD, :]
    tail_ref[0] = tail
    if nt > 1:
        win_ref[0:CONF_PAD, :] = tail


def _conformer(proj, conv0, w, b, lg, lb, nb, seq, out_dtype):
    m = proj.shape[0]
    tt = _tile(seq, 256)
    nt = seq // tt
    assert nt == 1 or tt >= CONF_PAD
    body = functools.partial(_conf_body, tt=tt, nt=nt)
    const = lambda shape: pl.BlockSpec(shape, lambda bb, t: (0,) * len(shape))
    return pl.pallas_call(
        body,
        grid=(nb, nt),
        in_specs=[pl.BlockSpec((tt, 2 * C_CONF), lambda bb, t: (bb * nt + t, COL_GLU // (2 * C_CONF))),
                  pl.BlockSpec((1, CONF_PAD, C_CONF), lambda bb, t: (bb, 0, 0)),
                  const((CONF_W, C_CONF)), const((1, C_CONF)), const((1, C_CONF)), const((1, C_CONF))],
        out_specs=[pl.BlockSpec((tt, C_CONF), lambda bb, t: (bb * nt + t, 0)),
                   pl.BlockSpec((1, CONF_PAD, C_CONF), lambda bb, t: (bb, 0, 0))],
        out_shape=[SDS((m, C_CONF), out_dtype), SDS((nb, CONF_PAD, C_CONF), F32)],
        scratch_shapes=[pltpu.VMEM((CONF_PAD + tt, C_CONF), F32),
                        pltpu.VMEM((SUBLANES - 1, tt + SUBLANES, C_CONF), F32)],
        compiler_params=_cparams(("parallel", "arbitrary"),
                                 [((tt, 2 * C_CONF), F32, 8), ((SUBLANES, tt + SUBLANES, C_CONF), F32, 1)]),
        name="conformer",
    )(proj, conv0, w, b, lg, lb)


def _outproj_body(x_ref, a_ref, y_ref, u_ref, w_ref, g_ref, x1_ref, h_ref):
    tm = x_ref.shape[0]
    halves = [slice(0, tm // 2), slice(tm // 2, tm)] if tm % (4 * SUBLANES) == 0 else [slice(0, tm)]
    prods = []
    for rs in halves:
        p = jnp.dot(a_ref[rs, :].astype(BF16), w_ref[0:D_ATT, :], preferred_element_type=F32)
        p = p + jnp.dot(y_ref[rs, :].astype(BF16), w_ref[D_ATT:D_ATT + D_SSD, :], preferred_element_type=F32)
        p = p + jnp.dot(u_ref[rs, :].astype(BF16), w_ref[D_ATT + D_SSD:, :], preferred_element_type=F32)
        prods.append(p)
    for rs, p in zip(halves, prods):
        acc = x_ref[rs, :] + p
        x1_ref[rs, :] = acc
        ms = jnp.mean(acc * acc, axis=-1, keepdims=True)
        h_ref[rs, :] = ((acc * lax.rsqrt(ms + EPS)) * g_ref[...]).astype(BF16)


def _out_proj(x, att, y, u, w, g):
    m, d = x.shape
    tm = _tile(m, 512)
    row = lambda width: pl.BlockSpec((tm, width), lambda i: (i, 0))
    return pl.pallas_call(
        _outproj_body,
        grid=(m // tm,),
        in_specs=[row(d), row(D_ATT), row(D_SSD), row(C_CONF),
                  pl.BlockSpec((d, d), lambda i: (0, 0)), pl.BlockSpec((1, d), lambda i: (0, 0))],
        out_specs=[row(d), row(d)],
        out_shape=[SDS((m, d), F32), SDS((m, d), BF16)],
        compiler_params=_cparams(("parallel",), [((tm, d), F32, 7), ((d, d), BF16, 2)]),
        name="out_proj",
    )(x, att, y, u, w, g)


HALO = 16


def _ffn_up_body(*refs, tm, seq, sample):
    if sample:
        h_ref, wg_ref, wv_ref, cw_ref, cb_ref, s1_ref, s2_ref, act_ref, tail_ref, lhs_ref = refs
    else:
        h_ref, halo_ref, wg_ref, wv_ref, cw_ref, cb_ref, act_ref, tail_ref, lhs_ref = refs
    i = pl.program_id(0)

    @pl.when(pl.program_id(1) == 0)
    def _():
        lhs_ref[HALO:, :] = h_ref[...]
        if sample:
            lhs_ref[0:HALO, :] = jnp.zeros((HALO, D_MODEL), BF16)
        else:
            lhs_ref[0:HALO, :] = halo_ref[...]

            @pl.when((i * tm) % seq == 0)
            def _():
                lhs_ref[0:HALO, :] = jnp.zeros((HALO, D_MODEL), BF16)

    gate = jnp.dot(lhs_ref[...], wg_ref[...], preferred_element_type=F32)
    val = jnp.dot(lhs_ref[HALO:, :], wv_ref[...], preferred_element_type=F32)
    g0 = gate[HALO:HALO + tm, :]
    g1 = gate[HALO - 1:HALO - 1 + tm, :]
    g2 = gate[HALO - 2:HALO - 2 + tm, :]
    if sample:
        tok = lax.broadcasted_iota(jnp.int32, (tm, 1), 0) & (seq - 1)
        g1 = jnp.where(tok < 1, s1_ref[...], g1)
        g2 = jnp.where(tok < 2, s2_ref[...], g2)
        tail_ref[...] = g0
    else:
        tail_ref[...] = g0[tm - SUBLANES:, :]
    pre = cw_ref[0:1, :] * g2 + cw_ref[1:2, :] * g1 + cw_ref[2:3, :] * g0 + cb_ref[...]
    act_ref[...] = ((pre * _sigmoid(pre)) * val).astype(BF16)


def _ffn_up(h, wg, wv, cw, cb, seq, s1=None, s2=None):
    m, d = h.shape
    f = wg.shape[1]
    sample = s1 is not None
    tm, tn = _tile(m, m if sample else min(1024, seq)), FFN_TN
    if sample:
        assert seq & (seq - 1) == 0 and seq >= FFN_CONV_W - 1
    else:
        assert seq % tm == 0
    body = functools.partial(_ffn_up_body, tm=tm, seq=seq, sample=sample)
    hspec = pl.BlockSpec((tm, d), lambda i, j: (i, 0))
    wspec = pl.BlockSpec((d, tn), lambda i, j: (0, j))
    cspec = lambda r: pl.BlockSpec((r, tn), lambda i, j: (0, j))
    tile = pl.BlockSpec((tm, tn), lambda i, j: (i, j))
    if sample:
        in_specs = [hspec, wspec, wspec, cspec(FFN_CONV_W), cspec(1), tile, tile]
        args = (h, wg, wv, cw, cb, s1, s2)
        tail_rows, tail_spec = m, tile
    else:
        halo = pl.BlockSpec((HALO, d), lambda i, j: (jnp.maximum(i * (tm // HALO) - 1, 0), 0))
        in_specs = [hspec, halo, wspec, wspec, cspec(FFN_CONV_W), cspec(1)]
        args = (h, h, wg, wv, cw, cb)
        tail_rows, tail_spec = (m // tm) * SUBLANES, pl.BlockSpec((SUBLANES, tn), lambda i, j: (i, j))
    return pl.pallas_call(
        body,
        grid=(m // tm, pl.cdiv(f, tn)),
        in_specs=in_specs,
        out_specs=[tile, tail_spec],
        out_shape=[SDS((m, f), BF16), SDS((tail_rows, f), F32)],
        scratch_shapes=[pltpu.VMEM((HALO + tm, d), BF16)],
        compiler_params=_cparams(("parallel", "arbitrary"),
                                 [((tm, d), BF16, 3), ((d, tn), BF16, 4), ((tm, tn), F32, 8)]),
        name="ffn_up",
    )(*args)


def _ffn_down_body(x_ref, a_ref, w_ref, o_ref):
    o_ref[...] = x_ref[...] + jnp.dot(a_ref[...], w_ref[...], preferred_element_type=F32)


def _ffn_down(x, act, w):
    m, d = x.shape
    f = act.shape[1]
    tm, tn = _tile(m, 1024), _tile(d, 512)
    return pl.pallas_call(
        _ffn_down_body,
        grid=(m // tm, d // tn),
        in_specs=[pl.BlockSpec((tm, tn), lambda i, j: (i, j)),
                  pl.BlockSpec((tm, f), lambda i, j: (i, 0)),
                  pl.BlockSpec((f, tn), lambda i, j: (0, j))],
        out_specs=pl.BlockSpec((tm, tn), lambda i, j: (i, j)),
        out_shape=SDS((m, d), F32),
        compiler_params=_cparams(("parallel", "arbitrary"),
                                 [((tm, f), BF16, 2), ((f, tn), BF16, 2), ((tm, tn), F32, 6)]),
        name="ffn_down",
    )(x, act, w)


def _final_norm_body(x_ref, g_ref, o_ref):
    x = x_ref[...]
    ms = jnp.mean(x * x, axis=-1, keepdims=True)
    o_ref[...] = (x * lax.rsqrt(ms + EPS)) * g_ref[...]


def _final_norm(x, g):
    m, d = x.shape
    tm = _tile(m, 512)
    return pl.pallas_call(
        _final_norm_body,
        grid=(m // tm,),
        in_specs=[pl.BlockSpec((tm, d), lambda i: (i, 0)), pl.BlockSpec((1, d), lambda i: (0, 0))],
        out_specs=pl.BlockSpec((tm, d), lambda i: (i, 0)),
        out_shape=SDS((m, d), F32),
        compiler_params=_cparams(("parallel",), [((tm, d), F32, 6)]),
        name="final_norm",
    )(x, g)


def _pad_cols(x, width):
    return jnp.pad(x, ((0, 0), (0, width - x.shape[1])))


def _layer_weights(l, p):
    w_in = p["w_in"][l]
    c_z, c_x, c_dt = 3 * D_ATT, 3 * D_ATT + D_SSD, 3 * D_ATT + D_SSD + SSD_CONV_DIM
    c_glu = c_dt + H_SSD
    order = [w_in[:, c_z:c_z + D_SSD],
             w_in[:, c_x:c_x + D_SSD],
             w_in[:, c_glu:c_glu + 2 * C_CONF],
             w_in[:, c_x + D_SSD:c_dt],
             w_in[:, 0:3 * D_ATT]]
    w_up = p["w_up"][l]
    row = lambda v: v.reshape(1, -1)
    return dict(
        w_main=jnp.concatenate(order, axis=1).astype(BF16),
        w_dt=_pad_cols(w_in[:, c_dt:c_dt + H_SSD], LANES).astype(BF16),
        g_mix=row(p["norm_mix_g"][l]),
        att_bias=p["att_logit_bias"][l],
        att_g=row(p["att_norm_g"][l]),
        ssd_cw=p["ssd_conv_w"][l], ssd_cb=row(p["ssd_conv_b"][l]),
        dtb=_pad_cols(row(p["ssd_dt_bias"][l]), LANES), alog=_pad_cols(row(p["ssd_a_log"][l]), LANES),
        dd=row(jnp.repeat(p["ssd_d"][l], P_SSD)), ssd_ng=row(p["ssd_norm_g"][l]),
        conf_w=p["conf_conv_w"][l], conf_b=row(p["conf_conv_b"][l]),
        conf_lg=row(p["conf_ln_g"][l]), conf_lb=row(p["conf_ln_b"][l]),
        w_out=p["w_out"][l].astype(BF16), g_ffn=row(p["norm_ffn_g"][l]),
        w_gate=w_up[:, :D_FF].astype(BF16), w_val=w_up[:, D_FF:].astype(BF16),
        ffn_cw=p["ffn_conv_w"][l], ffn_cb=row(p["ffn_conv_b"][l]),
        w_down=p["w_down"][l].astype(BF16),
    )


def _xbc_cols(proj3):
    return jnp.concatenate([proj3[..., COL_XS:COL_XS + D_SSD], proj3[..., COL_BC:COL_BC + 2 * G_SSD * N_SSD]],
                           axis=-1)


def kernel(x_prompt, x_sample, cache_k, cache_v, state_ssm, state_ssd_conv, state_conf_conv, state_ffn_conv, page_table, norm_mix_g, w_in, att_logit_bias, att_norm_g, ssd_conv_w, ssd_conv_b, ssd_dt_bias, ssd_a_log, ssd_d, ssd_norm_g, conf_conv_w, conf_conv_b, conf_ln_g, conf_ln_b, w_out, norm_ffn_g, w_up, ffn_conv_w, ffn_conv_b, w_down, norm_final_g):
    params = dict(norm_mix_g=norm_mix_g, w_in=w_in, att_logit_bias=att_logit_bias, att_norm_g=att_norm_g,
                  ssd_conv_w=ssd_conv_w, ssd_conv_b=ssd_conv_b, ssd_dt_bias=ssd_dt_bias, ssd_a_log=ssd_a_log,
                  ssd_d=ssd_d, ssd_norm_g=ssd_norm_g, conf_conv_w=conf_conv_w, conf_conv_b=conf_conv_b,
                  conf_ln_g=conf_ln_g, conf_ln_b=conf_ln_b, w_out=w_out, norm_ffn_g=norm_ffn_g, w_up=w_up,
                  ffn_conv_w=ffn_conv_w, ffn_conv_b=ffn_conv_b, w_down=w_down)
    depth = w_in.shape[0]
    bp, sp, d = x_prompt.shape
    bs, ss, _ = x_sample.shape
    assert ss == SUBLANES and d == D_MODEL
    n_pool = cache_k.shape[1]
    xp = x_prompt.reshape(bp * sp, d)
    xs = x_sample.reshape(bs * ss, d)
    uo = _cumsum_matrix()
    tri = jnp.tril(jnp.ones((SSD_CHUNK, SSD_CHUNK), F32)).astype(BF16)
    zeros_ssd_conv = jnp.zeros((bp, SUBLANES, SSD_CONV_DIM), F32)
    zeros_conf_conv = jnp.zeros((bp, CONF_PAD, C_CONF), F32)
    outs_p, outs_s = [], []
    for l in range(depth):
        w = _layer_weights(l, params)

        proj, dtp = _in_proj(xp, w["g_mix"], w["w_main"], w["w_dt"])
        att, k_t, v_t = _att_prompt(proj, w["att_bias"], w["att_g"], uo, bp, sp)
        y, h_t = _ssd(proj, dtp, zeros_ssd_conv, None, w["ssd_cw"], w["ssd_cb"], w["dtb"], w["alog"], w["dd"],
                      w["ssd_ng"], tri, bp, sp, BF16)
        u, conf_tail = _conformer(proj, zeros_conf_conv, w["conf_w"], w["conf_b"], w["conf_lg"], w["conf_lb"],
                                  bp, sp, BF16)
        x1, h2 = _out_proj(xp, att, y, u, w["w_out"], w["g_ffn"])
        act, gate_tail = _ffn_up(h2, w["w_gate"], w["w_val"], w["ffn_cw"], w["ffn_cb"], sp)
        xp = _ffn_down(x1, act, w["w_down"])
        proj3 = proj.reshape(bp, sp, D_PROJ)
        tiles_per_seq = gate_tail.shape[0] // SUBLANES // bp
        gt = gate_tail.reshape(bp, tiles_per_seq, SUBLANES, D_FF)
        outs_p.append((
            k_t, v_t,
            h_t,
            _xbc_cols(proj3[:, sp - (SSD_CONV_W - 1):]),
            conf_tail[:, CONF_PAD - (CONF_W - 1):],
            gt[:, -1, SUBLANES - (FFN_CONV_W - 1):]))

        proj, dtp = _in_proj(xs, w["g_mix"], w["w_main"], w["w_dt"])
        bias_rows = jnp.broadcast_to(jnp.repeat(w["att_bias"], SUBLANES)[:, None], (H_ATT * SUBLANES, TK_ATT))
        att = _att_sample(proj, cache_k, cache_v, page_table, l, n_pool, bias_rows, w["att_g"], uo)
        conv0 = jnp.pad(state_ssd_conv[l], ((0, 0), (SUBLANES - (SSD_CONV_W - 1), 0), (0, 0)))
        y, h_t = _ssd(proj, dtp, conv0, state_ssm[l], w["ssd_cw"], w["ssd_cb"], w["dtb"],
                      w["alog"], w["dd"], w["ssd_ng"], tri, bs, ss, F32)
        conf0 = jnp.pad(state_conf_conv[l], ((0, 0), (CONF_PAD - (CONF_W - 1), 0), (0, 0)))
        u, conf_tail = _conformer(proj, conf0, w["conf_w"], w["conf_b"], w["conf_lg"], w["conf_lb"], bs, ss, F32)
        x1, h2 = _out_proj(xs, att, y, u, w["w_out"], w["g_ffn"])
        st = state_ffn_conv[l]
        s1 = jnp.concatenate([st[:, 1:2], jnp.zeros((bs, ss - 1, D_FF), F32)], axis=1).reshape(bs * ss, -1)
        s2 = jnp.concatenate([st, jnp.zeros((bs, ss - 2, D_FF), F32)], axis=1).reshape(bs * ss, -1)
        act, gate_full = _ffn_up(h2, w["w_gate"], w["w_val"], w["ffn_cw"], w["ffn_cb"], ss, s1, s2)
        xs = _ffn_down(x1, act, w["w_down"])
        proj3 = proj.reshape(bs, ss, D_PROJ)
        outs_s.append((
            proj3[..., COL_K:COL_K + D_ATT].reshape(bs, ss, H_ATT, D_HEAD_ATT),
            proj3[..., COL_V:COL_V + D_ATT].reshape(bs, ss, H_ATT, D_HEAD_ATT),
            h_t,
            _xbc_cols(proj3[:, ss - (SSD_CONV_W - 1):]),
            conf_tail[:, CONF_PAD - (CONF_W - 1):],
            gate_full.reshape(bs, ss, D_FF)[:, ss - (FFN_CONV_W - 1):]))

    g_final = norm_final_g.reshape(1, d)
    y_prompt = _final_norm(xp, g_final).reshape(bp, sp, d)
    y_sample = _final_norm(xs, g_final).reshape(bs, ss, d)
    stack = lambda outs, i: jnp.stack([o[i] for o in outs])

    def kv_prompt(i):
        t = stack(outs_p, i).reshape(depth, bp, H_ATT, D_HEAD_ATT, sp)
        return jnp.transpose(t, (0, 1, 4, 2, 3))

    return (y_prompt, y_sample,
            kv_prompt(0), kv_prompt(1), stack(outs_s, 0), stack(outs_s, 1),
            stack(outs_p, 2), stack(outs_s, 2), stack(outs_p, 3), stack(outs_s, 3),
            stack(outs_p, 4), stack(outs_s, 4), stack(outs_p, 5), stack(outs_s, 5))
```

```python
import functools
import math

import jax
import jax.numpy as jnp
from jax import lax
from jax.experimental import pallas as pl
from jax.experimental.pallas import tpu as pltpu

F32 = jnp.float32
BF16 = jnp.bfloat16
SDS = jax.ShapeDtypeStruct

D_MODEL = 2048
D_HEAD_ATT = 64
D_ATT = 512
H_ATT = 8
D_SSD = 1024
P_SSD = 64
H_SSD = 16
G_SSD = 2
N_SSD = 128
SSD_CONV_W = 4
SSD_CONV_DIM = D_SSD + 2 * G_SSD * N_SSD
SSD_CHUNK = 128
C_CONF = 512
CONF_W = 31
D_FF = 5504
FFN_CONV_W = 3
PAGE_SIZE = 128
EPS = 1e-6

LANES = 128
SUBLANES = 8
VMEM_BYTES_V7X = 64 * 1024 * 1024
VMEM_LIMIT_MAX = VMEM_BYTES_V7X - 8 * 1024 * 1024
VMEM_COMPILER_SCRATCH = 16 * 1024 * 1024

D_PROJ = 5120
COL_Z, COL_XS, COL_GLU, COL_BC, COL_Q, COL_K, COL_V = 0, 1024, 2048, 3072, 3584, 4096, 4608
FFN_TN = 512
CONF_PAD = 32
TK_ATT = 128


def _nbytes(shape, dtype):
    return math.prod(shape) * jnp.dtype(dtype).itemsize


def _cparams(sem, bufs):
    est = sum(_nbytes(s, d) * n for s, d, n in bufs)
    return pltpu.CompilerParams(dimension_semantics=sem,
                                vmem_limit_bytes=int(min(est + VMEM_COMPILER_SCRATCH, VMEM_LIMIT_MAX)))


def _tile(m, pref):
    t = min(m, pref)
    assert m % t == 0, (m, pref)
    return t


def _sigmoid(x):
    return 1.0 / (1.0 + jnp.exp(-x))


def _softplus(x):
    return jnp.maximum(x, 0.0) + jnp.log1p(jnp.exp(-jnp.abs(x)))


def _split3(x):
    a = x.astype(BF16)
    r = x - a.astype(F32)
    b = r.astype(BF16)
    c = (r - b.astype(F32)).astype(BF16)
    return a, b, c


def _inproj_body(x_ref, g_ref, w_ref, wdt_ref, o_ref, dt_ref, hn_ref):
    @pl.when(pl.program_id(1) == 0)
    def _():
        x = x_ref[...]
        ms = jnp.mean(x * x, axis=-1, keepdims=True)
        hn_ref[...] = ((x * lax.rsqrt(ms + EPS)) * g_ref[...]).astype(BF16)
        dt_ref[...] = jnp.dot(hn_ref[...], wdt_ref[...], preferred_element_type=F32)

    o_ref[...] = jnp.dot(hn_ref[...], w_ref[...], preferred_element_type=F32)


def _in_proj(x, g, w, wdt):
    m, d = x.shape
    n = w.shape[1]
    tm, tn = _tile(m, 1024), _tile(n, 1024)
    return pl.pallas_call(
        _inproj_body,
        grid=(m // tm, n // tn),
        in_specs=[pl.BlockSpec((tm, d), lambda i, j: (i, 0)),
                  pl.BlockSpec((1, d), lambda i, j: (0, 0)),
                  pl.BlockSpec((d, tn), lambda i, j: (0, j)),
                  pl.BlockSpec((d, LANES), lambda i, j: (0, 0))],
        out_specs=[pl.BlockSpec((tm, tn), lambda i, j: (i, j)),
                   pl.BlockSpec((tm, LANES), lambda i, j: (i, 0))],
        out_shape=[SDS((m, n), F32), SDS((m, LANES), F32)],
        scratch_shapes=[pltpu.VMEM((tm, d), BF16)],
        compiler_params=_cparams(("parallel", "arbitrary"),
                                 [((tm, d), F32, 3), ((tm, d), BF16, 1), ((d, tn), BF16, 2),
                                  ((tm, tn), F32, 3), ((tm, LANES), F32, 2), ((d, LANES), BF16, 2)]),
        name="in_proj",
    )(x, g, w, wdt)


_NT = (((1,), (1,)), ((), ()))
LOG2E = 1.4426950408889634


def _sb_logs(z):
    neg_abs = lax.bitcast_convert_type(lax.bitcast_convert_type(z, jnp.uint32) | jnp.uint32(0x80000000), F32)
    lbeta = jnp.minimum(z, 0.0) - jnp.log2(1.0 + jnp.exp2(neg_abs))
    return lbeta, lbeta - z


def _sb_update(qm, kblk, vblk, bias, uo, carry, acc, mask):
    z = lax.dot_general(qm, kblk, _NT, preferred_element_type=F32) + bias
    lbeta, lk = _sb_logs(z)
    lk = jnp.where(mask, lk, 0.0)
    r = jnp.dot(lk.astype(BF16), uo, preferred_element_type=F32)
    w = jnp.where(mask, jnp.exp2(lbeta + r[:, :TK_ATT] + carry), 0.0)
    acc = acc + jnp.dot(w.astype(BF16), vblk, preferred_element_type=F32)
    return carry + r[:, TK_ATT:], acc


def _add_from_row(x, r0, delta):
    if r0 == 0:
        return x + delta
    return jnp.concatenate([x[:r0, :], x[r0:, :] + delta], axis=0)


def _cumsum_matrix():
    jp = lax.broadcasted_iota(jnp.int32, (TK_ATT, 2 * TK_ATT), 0)
    j = lax.broadcasted_iota(jnp.int32, (TK_ATT, 2 * TK_ATT), 1)
    return jnp.where((jp > j) | (j >= TK_ATT), 1.0, 0.0).astype(BF16)


def _att_prompt_body(bias_ref, q_ref, k_ref, v_ref, g_ref, uo_ref, o_ref, kt_ref, vt_ref, carry_ref, acc_ref,
                     *, tq, scale):
    hp = pl.program_id(1)
    qi = pl.program_id(2)

    @pl.when(qi == 0)
    def _():
        for blk in range(k_ref.shape[0] // LANES):
            rows = slice(blk * LANES, (blk + 1) * LANES)
            kt_ref[0, :, rows] = k_ref[rows, :].T
            vt_ref[0, :, rows] = v_ref[rows, :].T

    nsub = tq // TK_ATT
    lane = lax.broadcasted_iota(jnp.int32, (1, LANES), 1)
    low = lane < D_HEAD_ATT
    uo = uo_ref[...]
    qb = (q_ref[...] * (scale * LOG2E)).astype(BF16)
    lane2 = lax.broadcasted_iota(jnp.int32, (1, 2 * TK_ATT), 1)
    bias = jnp.where(lane2 < TK_ATT, bias_ref[2 * hp], bias_ref[2 * hp + 1]) * LOG2E
    qrow = lax.broadcasted_iota(jnp.int32, (tq, 1), 0)
    carry_ref[...] = jnp.zeros_like(carry_ref)
    acc_ref[...] = jnp.zeros_like(acc_ref)

    def span_step(ks, nblk, masked):
        k0 = pl.multiple_of(ks * tq, tq)
        carries = [carry_ref[0], carry_ref[1]]
        accs = [acc_ref[0], acc_ref[1]]
        order = list(reversed(range(nblk)))
        zs, vblks, masks, row0s = [], [], [], []
        for sb in order:
            rows = pl.ds(k0 + sb * TK_ATT, TK_ATT)
            r0 = sb * TK_ATT if masked else 0
            kf = k_ref[rows, :]
            kbd = jnp.concatenate([jnp.where(low, kf, 0.0), jnp.where(low, 0.0, kf)], axis=0).astype(BF16)
            vblks.append(v_ref[rows, :].astype(BF16))
            zs.append(lax.dot_general(qb[r0:, :], kbd, _NT, preferred_element_type=F32) + bias)
            masks.append((sb * TK_ATT + (lane2 & (TK_ATT - 1)) < qrow[r0:, :]) if masked else None)
            row0s.append(r0)
        lbetas, rs = [], []
        for z, mask in zip(zs, masks):
            lbeta, lk = _sb_logs(z)
            if masked:
                lk = jnp.where(mask, lk, 0.0)
            lkb = lk.astype(BF16)
            lbetas.append(lbeta)
            rs.append([jnp.dot(lkb[:, hh * TK_ATT:(hh + 1) * TK_ATT], uo, preferred_element_type=F32)
                       for hh in range(2)])
        for lbeta, r2, mask, vblk, r0 in zip(lbetas, rs, masks, vblks, row0s):
            for hh in range(2):
                hs = slice(hh * TK_ATT, (hh + 1) * TK_ATT)
                w = jnp.exp2(lbeta[:, hs] + r2[hh][:, :TK_ATT] + carries[hh][r0:, :])
                if masked:
                    w = jnp.where(mask[:, hs], w, 0.0)
                pv = jnp.dot(w.astype(BF16), vblk, preferred_element_type=F32)
                accs[hh] = _add_from_row(accs[hh], r0, pv)
                carries[hh] = _add_from_row(carries[hh], r0, r2[hh][:, TK_ATT:])
        for hh in range(2):
            carry_ref[hh] = carries[hh]
            acc_ref[hh] = accs[hh]

    span_step(qi, nsub, True)

    def off_diag(t, _):
        span_step(qi - 1 - t, nsub, False)
        return 0

    lax.fori_loop(0, qi, off_diag, 0)
    o = jnp.where(low, acc_ref[0], acc_ref[1])
    sq = o * o
    s0 = jnp.sum(jnp.where(low, sq, 0.0), axis=-1, keepdims=True)
    s1 = jnp.sum(jnp.where(low, 0.0, sq), axis=-1, keepdims=True)
    ms = jnp.where(low, s0, s1) * (1.0 / D_HEAD_ATT)
    o_ref[...] = ((o * lax.rsqrt(ms + EPS)) * g_ref[...]).astype(o_ref.dtype)


def _att_prompt(proj, bias, g, uo, nb, seq):
    m = proj.shape[0]
    tq = _tile(seq, 512)
    nq = seq // tq
    npair = H_ATT // 2
    body = functools.partial(_att_prompt_body, tq=tq, scale=D_HEAD_ATT ** -0.5)
    return pl.pallas_call(
        body,
        grid=(nb, npair, nq),
        in_specs=[pl.BlockSpec(memory_space=pltpu.SMEM),
                  pl.BlockSpec((tq, LANES), lambda b, h, i: (b * nq + i, COL_Q // LANES + h)),
                  pl.BlockSpec((seq, LANES), lambda b, h, i: (b, COL_K // LANES + h)),
                  pl.BlockSpec((seq, LANES), lambda b, h, i: (b, COL_V // LANES + h)),
                  pl.BlockSpec((1, LANES), lambda b, h, i: (0, h)),
                  pl.BlockSpec((TK_ATT, 2 * TK_ATT), lambda b, h, i: (0, 0))],
        out_specs=[pl.BlockSpec((tq, LANES), lambda b, h, i: (b * nq + i, h)),
                   pl.BlockSpec((1, LANES, seq), lambda b, h, i: (b, h, 0)),
                   pl.BlockSpec((1, LANES, seq), lambda b, h, i: (b, h, 0))],
        out_shape=[SDS((m, D_ATT), BF16), SDS((nb, D_ATT, seq), F32), SDS((nb, D_ATT, seq), F32)],
        scratch_shapes=[pltpu.VMEM((2, tq, TK_ATT), F32), pltpu.VMEM((2, tq, LANES), F32)],
        compiler_params=_cparams(("parallel", "parallel", "arbitrary"),
                                 [((seq, LANES), F32, 8), ((tq, LANES), F32, 24)]),
        name="att_prompt",
    )(bias, proj, proj, proj, g, uo)


def _att_sample_body(pt_ref, q_ref, kn_ref, vn_ref, bias_ref, g_ref, uo_ref, *rest, npg, scale):
    k_refs, v_refs = rest[:npg], rest[npg:2 * npg]
    o_ref, qbd_ref, kpad_ref, vpad_ref, carry_ref, acc_ref = rest[2 * npg:]
    del pt_ref
    p = pl.program_id(1)
    nrow = H_ATT * SUBLANES
    row = lax.broadcasted_iota(jnp.int32, (nrow, D_ATT), 0)
    lane = lax.broadcasted_iota(jnp.int32, (nrow, D_ATT), 1)
    own = (row >> 3) == (lane >> 6)
    uo = uo_ref[...]
    bias = bias_ref[...] * LOG2E

    @pl.when(p == 0)
    def _():
        q = q_ref[...] * (scale * LOG2E)
        qt = jnp.concatenate([q] * H_ATT, axis=0)
        qbd = jnp.where(own, qt, 0.0).astype(BF16)
        qbd_ref[...] = qbd
        kpad_ref[...] = jnp.zeros_like(kpad_ref)
        vpad_ref[...] = jnp.zeros_like(vpad_ref)
        kpad_ref[0:SUBLANES, :] = kn_ref[...]
        vpad_ref[0:SUBLANES, :] = vn_ref[...]
        tok = lax.broadcasted_iota(jnp.int32, (nrow, 1), 0) & (SUBLANES - 1)
        mask = lax.broadcasted_iota(jnp.int32, (1, TK_ATT), 1) < tok
        carry, acc = _sb_update(qbd, kpad_ref[...].astype(BF16), vpad_ref[...].astype(BF16), bias, uo,
                                jnp.zeros((nrow, TK_ATT), F32), jnp.zeros((nrow, D_ATT), F32), mask)
        carry_ref[...] = carry
        acc_ref[...] = acc

    carry, acc = carry_ref[...], acc_ref[...]
    qbd = qbd_ref[...]
    zs = [jnp.dot(qbd, k_refs[gi][0].astype(BF16), preferred_element_type=F32) + bias for gi in range(npg)]
    lbetas, rs = [], []
    for z in zs:
        lbeta, lk = _sb_logs(z)
        lbetas.append(lbeta)
        rs.append(jnp.dot(lk.astype(BF16), uo, preferred_element_type=F32))
    for gi in range(npg):
        w = jnp.exp2(lbetas[gi] + rs[gi][:, :TK_ATT] + carry)
        acc = acc + lax.dot_general(w.astype(BF16), v_refs[gi][0].astype(BF16), _NT, preferred_element_type=F32)
        carry = carry + rs[gi][:, TK_ATT:]
    carry_ref[...] = carry
    acc_ref[...] = acc

    @pl.when(p == pl.num_programs(1) - 1)
    def _():
        a = jnp.where(own, acc, 0.0)
        ms = jnp.sum(a * a, axis=-1, keepdims=True) * (1.0 / D_HEAD_ATT)
        a = a * lax.rsqrt(ms + EPS)
        o = a[0:SUBLANES]
        for h in range(1, H_ATT):
            o = o + a[h * SUBLANES:(h + 1) * SUBLANES]
        o_ref[...] = o * g_ref[...]


def _att_sample(proj, cache_k, cache_v, page_table, layer, n_pool, bias_rows, g, uo):
    m = proj.shape[0]
    nseq, npages = page_table.shape
    assert m == nseq * SUBLANES
    npg = next(n for n in (32, 16, 8, 4, 2, 1) if npages % n == 0)
    pt = page_table.reshape(-1)
    ck = jnp.transpose(cache_k, (0, 1, 3, 4, 2)).reshape(-1, D_ATT, PAGE_SIZE)
    cv = jnp.transpose(cache_v, (0, 1, 3, 4, 2)).reshape(-1, D_ATT, PAGE_SIZE)
    nrow = H_ATT * SUBLANES

    def page_map(gi):
        def f(b, p, pt_ref):
            return (layer * n_pool + pt_ref[b * npages + (npages - 1 - (p * npg + gi))], 0, 0)
        return f

    page_specs = [pl.BlockSpec((1, D_ATT, PAGE_SIZE), page_map(gi)) for gi in range(npg)]
    body = functools.partial(_att_sample_body, npg=npg, scale=D_HEAD_ATT ** -0.5)
    grid_spec = pltpu.PrefetchScalarGridSpec(
        num_scalar_prefetch=1,
        grid=(nseq, npages // npg),
        in_specs=[pl.BlockSpec((SUBLANES, D_ATT), lambda b, p, pt_ref: (b, COL_Q // D_ATT)),
                  pl.BlockSpec((SUBLANES, D_ATT), lambda b, p, pt_ref: (b, COL_K // D_ATT)),
                  pl.BlockSpec((SUBLANES, D_ATT), lambda b, p, pt_ref: (b, COL_V // D_ATT)),
                  pl.BlockSpec((nrow, TK_ATT), lambda b, p, pt_ref: (0, 0)),
                  pl.BlockSpec((1, D_ATT), lambda b, p, pt_ref: (0, 0)),
                  pl.BlockSpec((TK_ATT, 2 * TK_ATT), lambda b, p, pt_ref: (0, 0))] + page_specs + page_specs,
        out_specs=pl.BlockSpec((SUBLANES, D_ATT), lambda b, p, pt_ref: (b, 0)),
        scratch_shapes=[pltpu.VMEM((nrow, D_ATT), BF16),
                        pltpu.VMEM((TK_ATT, D_ATT), F32),
                        pltpu.VMEM((TK_ATT, D_ATT), F32),
                        pltpu.VMEM((nrow, TK_ATT), F32),
                        pltpu.VMEM((nrow, D_ATT), F32)])
    return pl.pallas_call(
        body,
        grid_spec=grid_spec,
        out_shape=SDS((m, D_ATT), F32),
        compiler_params=_cparams(("parallel", "arbitrary"),
                                 [((PAGE_SIZE, D_ATT), F32, 4 * npg), ((TK_ATT, D_ATT), F32, 4)]),
        name="att_sample",
    )(pt, proj, proj, proj, bias_rows, g, uo, *([ck] * npg), *([cv] * npg))


def _ssd_body(*refs, lr, nc, has_h0):
    if has_h0:
        (z_ref, xs_ref, bc_ref, dt_ref, c0_ref, h0_ref, cw_ref, cb_ref, dtb_ref, alog_ref, dd_ref, ng_ref,
         tri_ref, y_ref, ht_ref, win_ref, hs_ref, dts_ref, ysc_ref, xd_ref, et_ref) = refs
    else:
        (z_ref, xs_ref, bc_ref, dt_ref, c0_ref, cw_ref, cb_ref, dtb_ref, alog_ref, dd_ref, ng_ref,
         tri_ref, y_ref, ht_ref, win_ref, hs_ref, dts_ref, ysc_ref, xd_ref, et_ref) = refs
        h0_ref = None
    L = SSD_CHUNK
    c = pl.program_id(1)
    hist = SUBLANES
    pairs_per_group = H_SSD // G_SSD // 2

    @pl.when(c == 0)
    def _():
        win_ref[0:hist, :] = c0_ref[0]
        if lr < L:
            win_ref[hist:hist + L, :] = jnp.zeros((L, SSD_CONV_DIM), F32)
            dts_ref[...] = jnp.zeros_like(dts_ref)
        if has_h0:
            for hp in range(H_SSD // 2):
                pair = jnp.concatenate([h0_ref[0, 2 * hp], h0_ref[0, 2 * hp + 1]], axis=0)
                gsl = slice((hp % pairs_per_group) * LANES, (hp % pairs_per_group + 1) * LANES)
                hs_ref[hp // pairs_per_group, :, gsl] = pair.T
        else:
            hs_ref[...] = jnp.zeros_like(hs_ref)

    win_ref[hist:hist + lr, 0:D_SSD] = xs_ref[...]
    win_ref[hist:hist + lr, D_SSD:SSD_CONV_DIM] = bc_ref[...]
    dts_ref[0:lr, :] = dt_ref[...]

    lane = lax.broadcasted_iota(jnp.int32, (1, LANES), 1)
    low = lane < P_SSD
    rowi = lax.broadcasted_iota(jnp.int32, (L, 1), 0)
    dt = _softplus(dts_ref[...] + dtb_ref[...])
    if lr < L:
        dt = jnp.where(rowi < lr, dt, 0.0)
    a_head = jnp.where(lane < H_SSD, -jnp.exp(alog_ref[...]), 0.0)
    a = dt * a_head
    tri = tri_ref[...]
    a1, a2, a3 = _split3(a)
    a_cs = (jnp.dot(tri, a1, preferred_element_type=F32) + jnp.dot(tri, a2, preferred_element_type=F32)
            + jnp.dot(tri, a3, preferred_element_type=F32))
    a_cs_t = a_cs.T
    a_tot = a_cs[L - 1:L, :]
    causal = rowi >= lax.broadcasted_iota(jnp.int32, (1, L), 1)

    pre = cb_ref[...]
    for k in range(SSD_CONV_W):
        off = hist - (SSD_CONV_W - 1) + k
        pre = pre + cw_ref[k:k + 1, :] * win_ref[off:off + L, :]
    xbc = pre * _sigmoid(pre)
    win_ref[0:hist, :] = win_ref[L:L + hist, :]

    cbs, bts, cgs = [], [], []
    for g in range(G_SSD):
        bg = xbc[:, D_SSD + g * N_SSD:D_SSD + (g + 1) * N_SSD]
        cg = xbc[:, D_SSD + (G_SSD + g) * N_SSD:D_SSD + (G_SSD + g + 1) * N_SSD].astype(BF16)
        cbs.append(lax.dot_general(cg, bg.astype(BF16), (((1,), (1,)), ((), ())), preferred_element_type=F32))
        bts.append(bg.T.astype(BF16))
        cgs.append(cg)

    for hp in range(H_SSD // 2):
        g = hp // pairs_per_group
        cols, dtcols, mats = [], [], []
        for h in (2 * hp, 2 * hp + 1):
            col = jnp.broadcast_to(a_cs[:, h:h + 1], (L, LANES))
            dec = jnp.exp(jnp.where(causal, col - a_cs_t[h:h + 1, :], -1e30))
            mats.append((cbs[g] * dec).astype(BF16))
            cols.append(col)
            dtcols.append(jnp.broadcast_to(dt[:, h:h + 1], (L, LANES)))
        sl = slice(hp * LANES, (hp + 1) * LANES)
        xs_pair = xbc[:, sl]
        xdt = xs_pair * jnp.where(low, dtcols[0], dtcols[1])
        xdt_bf = xdt.astype(BF16)
        ydiag = jnp.where(low, jnp.dot(mats[0], xdt_bf, preferred_element_type=F32),
                          jnp.dot(mats[1], xdt_bf, preferred_element_type=F32))
        cs_pair = jnp.where(low, cols[0], cols[1])
        gsl = slice((hp % pairs_per_group) * LANES, (hp % pairs_per_group + 1) * LANES)
        yoff = jnp.dot(cgs[g], hs_ref[g, :, gsl].astype(BF16), preferred_element_type=F32) * jnp.exp(cs_pair)
        ysc_ref[:, sl] = ydiag + yoff + dd_ref[:, sl] * xs_pair
        tot_pair = jnp.where(low, jnp.broadcast_to(a_tot[:, 2 * hp:2 * hp + 1], (1, LANES)),
                             jnp.broadcast_to(a_tot[:, 2 * hp + 1:2 * hp + 2], (1, LANES)))
        xd_ref[:, sl] = (xdt * jnp.exp(tot_pair - cs_pair)).astype(BF16)
        et_ref[:, sl] = jnp.exp(tot_pair)

    gw = D_SSD // G_SSD
    for g in range(G_SSD):
        gs = slice(g * gw, (g + 1) * gw)
        cst = jnp.dot(bts[g], xd_ref[:, gs], preferred_element_type=F32)
        hs_ref[g] = hs_ref[g] * et_ref[:, gs] + cst

    zz = z_ref[...]
    y = ysc_ref[0:lr, :] * (zz * _sigmoid(zz))
    for g in range(G_SSD):
        gs = slice(g * gw, (g + 1) * gw)
        yg = y[:, gs]
        ms = jnp.mean(yg * yg, axis=-1, keepdims=True)
        y_ref[:, gs] = ((yg * lax.rsqrt(ms + EPS)) * ng_ref[:, gs]).astype(y_ref.dtype)

    @pl.when(c == nc - 1)
    def _():
        for hp in range(H_SSD // 2):
            gsl = slice((hp % pairs_per_group) * LANES, (hp % pairs_per_group + 1) * LANES)
            pair_t = hs_ref[hp // pairs_per_group, :, gsl].T
            ht_ref[0, 2 * hp] = pair_t[0:P_SSD]
            ht_ref[0, 2 * hp + 1] = pair_t[P_SSD:2 * P_SSD]


def _ssd(proj, dtp, conv0, h0, cw, cb, dtb, alog, dd, ng, tri, nb, seq, out_dtype):
    m = proj.shape[0]
    lr = min(seq, SSD_CHUNK)
    assert seq % lr == 0
    nc = seq // lr
    has_h0 = h0 is not None
    gw = D_SSD // G_SSD
    row = lambda b, c: b * nc + c
    in_specs = [pl.BlockSpec((lr, D_SSD), lambda b, c: (row(b, c), COL_Z // D_SSD)),
                pl.BlockSpec((lr, D_SSD), lambda b, c: (row(b, c), COL_XS // D_SSD)),
                pl.BlockSpec((lr, 2 * G_SSD * N_SSD), lambda b, c: (row(b, c), COL_BC // (2 * G_SSD * N_SSD))),
                pl.BlockSpec((lr, LANES), lambda b, c: (row(b, c), 0)),
                pl.BlockSpec((1, SUBLANES, SSD_CONV_DIM), lambda b, c: (b, 0, 0))]
    args = [proj, proj, proj, dtp, conv0]
    if has_h0:
        in_specs.append(pl.BlockSpec((1, H_SSD, P_SSD, N_SSD), lambda b, c: (b, 0, 0, 0)))
        args.append(h0)
    const = lambda shape: pl.BlockSpec(shape, lambda b, c: (0,) * len(shape))
    in_specs += [const((SSD_CONV_W, SSD_CONV_DIM)), const((1, SSD_CONV_DIM)), const((1, LANES)),
                 const((1, LANES)), const((1, D_SSD)), const((1, D_SSD)), const((SSD_CHUNK, SSD_CHUNK))]
    args += [cw, cb, dtb, alog, dd, ng, tri]
    body = functools.partial(_ssd_body, lr=lr, nc=nc, has_h0=has_h0)
    L = SSD_CHUNK
    return pl.pallas_call(
        body,
        grid=(nb, nc),
        in_specs=in_specs,
        out_specs=[pl.BlockSpec((lr, D_SSD), lambda b, c: (row(b, c), 0)),
                   pl.BlockSpec((1, H_SSD, P_SSD, N_SSD), lambda b, c: (b, 0, 0, 0))],
        out_shape=[SDS((m, D_SSD), out_dtype), SDS((nb, H_SSD, P_SSD, N_SSD), F32)],
        scratch_shapes=[pltpu.VMEM((L + SUBLANES, SSD_CONV_DIM), F32),
                        pltpu.VMEM((G_SSD, N_SSD, gw), F32),
                        pltpu.VMEM((L, LANES), F32),
                        pltpu.VMEM((L, D_SSD), F32),
                        pltpu.VMEM((L, D_SSD), BF16),
                        pltpu.VMEM((1, D_SSD), F32)],
        compiler_params=_cparams(("parallel", "arbitrary"),
                                 [((L, SSD_CONV_DIM), F32, 12), ((G_SSD, N_SSD, gw), F32, 5)]),
        name="ssd",
    )(*args)


def _conf_body(glu_ref, c0_ref, w_ref, b_ref, lg_ref, lb_ref, o_ref, tail_ref, win_ref, part_ref, *, tt, nt):
    t = pl.program_id(1)

    @pl.when(t == 0)
    def _():
        win_ref[0:CONF_PAD, :] = c0_ref[0]

    glu = glu_ref[...]
    win_ref[CONF_PAD:CONF_PAD + tt, :] = glu[:, :C_CONF] * _sigmoid(glu[:, C_CONF:])
    first = CONF_PAD - (CONF_W - 1)
    acc = jnp.broadcast_to(b_ref[...], (tt, C_CONF))
    for r in range(SUBLANES):
        offs = [j for j in range(r, first + CONF_W, SUBLANES) if j >= first]
        rows = tt if r == 0 else tt + SUBLANES
        part = w_ref[offs[0] - first:offs[0] - first + 1, :] * win_ref[offs[0] - r:offs[0] - r + rows, :]
        for j in offs[1:]:
            part = part + w_ref[j - first:j - first + 1, :] * win_ref[j - r:j - r + rows, :]
        if r == 0:
            acc = acc + part
        else:
            part_ref[r - 1] = part
            acc = acc + part_ref[r - 1, r:r + tt, :]
    mu = jnp.mean(acc, axis=-1, keepdims=True)
    xc = acc - mu
    var = jnp.mean(xc * xc, axis=-1, keepdims=True)
    yn = (xc * lax.rsqrt(var + EPS)) * lg_ref[...] + lb_ref[...]
    o_ref[...] = (yn * _sigmoid(yn)).astype(o_ref.dtype)
    tail = win_ref[tt:tt + CONF_PAD, :]
    tail_ref[0] = tail
    if nt > 1:
        win_ref[0:CONF_PAD, :] = tail


def _conformer(proj, conv0, w, b, lg, lb, nb, seq, out_dtype):
    m = proj.shape[0]
    tt = _tile(seq, 512)
    nt = seq // tt
    assert nt == 1 or tt >= CONF_PAD
    body = functools.partial(_conf_body, tt=tt, nt=nt)
    const = lambda shape: pl.BlockSpec(shape, lambda bb, t: (0,) * len(shape))
    return pl.pallas_call(
        body,
        grid=(nb, nt),
        in_specs=[pl.BlockSpec((tt, 2 * C_CONF), lambda bb, t: (bb * nt + t, COL_GLU // (2 * C_CONF))),
                  pl.BlockSpec((1, CONF_PAD, C_CONF), lambda bb, t: (bb, 0, 0)),
                  const((CONF_W, C_CONF)), const((1, C_CONF)), const((1, C_CONF)), const((1, C_CONF))],
        out_specs=[pl.BlockSpec((tt, C_CONF), lambda bb, t: (bb * nt + t, 0)),
                   pl.BlockSpec((1, CONF_PAD, C_CONF), lambda bb, t: (bb, 0, 0))],
        out_shape=[SDS((m, C_CONF), out_dtype), SDS((nb, CONF_PAD, C_CONF), F32)],
        scratch_shapes=[pltpu.VMEM((CONF_PAD + tt, C_CONF), F32),
                        pltpu.VMEM((SUBLANES - 1, tt + SUBLANES, C_CONF), F32)],
        compiler_params=_cparams(("parallel", "arbitrary"),
                                 [((tt, 2 * C_CONF), F32, 8), ((SUBLANES, tt + SUBLANES, C_CONF), F32, 1)]),
        name="conformer",
    )(proj, conv0, w, b, lg, lb)


def _outproj_body(x_ref, a_ref, y_ref, u_ref, w_ref, g_ref, x1_ref, h_ref):
    tm = x_ref.shape[0]
    halves = [slice(0, tm // 2), slice(tm // 2, tm)] if tm % (4 * SUBLANES) == 0 else [slice(0, tm)]
    prods = []
    for rs in halves:
        p = jnp.dot(a_ref[rs, :].astype(BF16), w_ref[0:D_ATT, :], preferred_element_type=F32)
        p = p + jnp.dot(y_ref[rs, :].astype(BF16), w_ref[D_ATT:D_ATT + D_SSD, :], preferred_element_type=F32)
        p = p + jnp.dot(u_ref[rs, :].astype(BF16), w_ref[D_ATT + D_SSD:, :], preferred_element_type=F32)
        prods.append(p)
    for rs, p in zip(halves, prods):
        acc = x_ref[rs, :] + p
        x1_ref[rs, :] = acc
        ms = jnp.mean(acc * acc, axis=-1, keepdims=True)
        h_ref[rs, :] = ((acc * lax.rsqrt(ms + EPS)) * g_ref[...]).astype(BF16)


def _out_proj(x, att, y, u, w, g):
    m, d = x.shape
    tm = _tile(m, 512)
    row = lambda width: pl.BlockSpec((tm, width), lambda i: (i, 0))
    return pl.pallas_call(
        _outproj_body,
        grid=(m // tm,),
        in_specs=[row(d), row(D_ATT), row(D_SSD), row(C_CONF),
                  pl.BlockSpec((d, d), lambda i: (0, 0)), pl.BlockSpec((1, d), lambda i: (0, 0))],
        out_specs=[row(d), row(d)],
        out_shape=[SDS((m, d), F32), SDS((m, d), BF16)],
        compiler_params=_cparams(("parallel",), [((tm, d), F32, 7), ((d, d), BF16, 2)]),
        name="out_proj",
    )(x, att, y, u, w, g)


HALO = 16


def _ffn_up_body(*refs, tm, seq, sample):
    if sample:
        h_ref, wg_ref, wv_ref, cw_ref, cb_ref, s1_ref, s2_ref, act_ref, tail_ref, lhs_ref = refs
    else:
        h_ref, halo_ref, wg_ref, wv_ref, cw_ref, cb_ref, act_ref, tail_ref, lhs_ref = refs
    i = pl.program_id(0)

    @pl.when(pl.program_id(1) == 0)
    def _():
        lhs_ref[HALO:, :] = h_ref[...]
        if sample:
            lhs_ref[0:HALO, :] = jnp.zeros((HALO, D_MODEL), BF16)
        else:
            lhs_ref[0:HALO, :] = halo_ref[...]

            @pl.when((i * tm) % seq == 0)
            def _():
                lhs_ref[0:HALO, :] = jnp.zeros((HALO, D_MODEL), BF16)

    gate = jnp.dot(lhs_ref[...], wg_ref[...], preferred_element_type=F32)
    val = jnp.dot(lhs_ref[HALO:, :], wv_ref[...], preferred_element_type=F32)
    g0 = gate[HALO:HALO + tm, :]
    g1 = gate[HALO - 1:HALO - 1 + tm, :]
    g2 = gate[HALO - 2:HALO - 2 + tm, :]
    if sample:
        tok = lax.broadcasted_iota(jnp.int32, (tm, 1), 0) & (seq - 1)
        g1 = jnp.where(tok < 1, s1_ref[...], g1)
        g2 = jnp.where(tok < 2, s2_ref[...], g2)
        tail_ref[...] = g0
    else:
        tail_ref[...] = g0[tm - SUBLANES:, :]
    pre = cw_ref[0:1, :] * g2 + cw_ref[1:2, :] * g1 + cw_ref[2:3, :] * g0 + cb_ref[...]
    act_ref[...] = ((pre * _sigmoid(pre)) * val).astype(BF16)


def _ffn_up(h, wg, wv, cw, cb, seq, s1=None, s2=None):
    m, d = h.shape
    f = wg.shape[1]
    sample = s1 is not None
    tm, tn = _tile(m, m if sample else min(1024, seq)), FFN_TN
    if sample:
        assert seq & (seq - 1) == 0 and seq >= FFN_CONV_W - 1
    else:
        assert seq % tm == 0
    body = functools.partial(_ffn_up_body, tm=tm, seq=seq, sample=sample)
    hspec = pl.BlockSpec((tm, d), lambda i, j: (i, 0))
    wspec = pl.BlockSpec((d, tn), lambda i, j: (0, j))
    cspec = lambda r: pl.BlockSpec((r, tn), lambda i, j: (0, j))
    tile = pl.BlockSpec((tm, tn), lambda i, j: (i, j))
    if sample:
        in_specs = [hspec, wspec, wspec, cspec(FFN_CONV_W), cspec(1), tile, tile]
        args = (h, wg, wv, cw, cb, s1, s2)
        tail_rows, tail_spec = m, tile
    else:
        halo = pl.BlockSpec((HALO, d), lambda i, j: (jnp.maximum(i * (tm // HALO) - 1, 0), 0))
        in_specs = [hspec, halo, wspec, wspec, cspec(FFN_CONV_W), cspec(1)]
        args = (h, h, wg, wv, cw, cb)
        tail_rows, tail_spec = (m // tm) * SUBLANES, pl.BlockSpec((SUBLANES, tn), lambda i, j: (i, j))
    return pl.pallas_call(
        body,
        grid=(m // tm, pl.cdiv(f, tn)),
        in_specs=in_specs,
        out_specs=[tile, tail_spec],
        out_shape=[SDS((m, f), BF16), SDS((tail_rows, f), F32)],
        scratch_shapes=[pltpu.VMEM((HALO + tm, d), BF16)],
        compiler_params=_cparams(("parallel", "arbitrary"),
                                 [((tm, d), BF16, 3), ((d, tn), BF16, 4), ((tm, tn), F32, 8)]),
        name="ffn_up",
    )(*args)


def _ffn_down_body(x_ref, a_ref, w_ref, o_ref):
    o_ref[...] = x_ref[...] + jnp.dot(a_ref[...], w_ref[...], preferred_element_type=F32)


def _ffn_down(x, act, w):
    m, d = x.shape
    f = act.shape[1]
    tm, tn = _tile(m, 1024), _tile(d, 512)
    return pl.pallas_call(
        _ffn_down_body,
        grid=(m // tm, d // tn),
        in_specs=[pl.BlockSpec((tm, tn), lambda i, j: (i, j)),
                  pl.BlockSpec((tm, f), lambda i, j: (i, 0)),
                  pl.BlockSpec((f, tn), lambda i, j: (0, j))],
        out_specs=pl.BlockSpec((tm, tn), lambda i, j: (i, j)),
        out_shape=SDS((m, d), F32),
        compiler_params=_cparams(("parallel", "arbitrary"),
                                 [((tm, f), BF16, 2), ((f, tn), BF16, 2), ((tm, tn), F32, 6)]),
        name="ffn_down",
    )(x, act, w)


def _final_norm_body(x_ref, g_ref, o_ref):
    x = x_ref[...]
    ms = jnp.mean(x * x, axis=-1, keepdims=True)
    o_ref[...] = (x * lax.rsqrt(ms + EPS)) * g_ref[...]


def _final_norm(x, g):
    m, d = x.shape
    tm = _tile(m, 512)
    return pl.pallas_call(
        _final_norm_body,
        grid=(m // tm,),
        in_specs=[pl.BlockSpec((tm, d), lambda i: (i, 0)), pl.BlockSpec((1, d), lambda i: (0, 0))],
        out_specs=pl.BlockSpec((tm, d), lambda i: (i, 0)),
        out_shape=SDS((m, d), F32),
        compiler_params=_cparams(("parallel",), [((tm, d), F32, 6)]),
        name="final_norm",
    )(x, g)


def _pad_cols(x, width):
    return jnp.pad(x, ((0, 0), (0, width - x.shape[1])))


def _layer_weights(l, p):
    w_in = p["w_in"][l]
    c_z, c_x, c_dt = 3 * D_ATT, 3 * D_ATT + D_SSD, 3 * D_ATT + D_SSD + SSD_CONV_DIM
    c_glu = c_dt + H_SSD
    order = [w_in[:, c_z:c_z + D_SSD],
             w_in[:, c_x:c_x + D_SSD],
             w_in[:, c_glu:c_glu + 2 * C_CONF],
             w_in[:, c_x + D_SSD:c_dt],
             w_in[:, 0:3 * D_ATT]]
    w_up = p["w_up"][l]
    row = lambda v: v.reshape(1, -1)
    return dict(
        w_main=jnp.concatenate(order, axis=1).astype(BF16),
        w_dt=_pad_cols(w_in[:, c_dt:c_dt + H_SSD], LANES).astype(BF16),
        g_mix=row(p["norm_mix_g"][l]),
        att_bias=p["att_logit_bias"][l],
        att_g=row(p["att_norm_g"][l]),
        ssd_cw=p["ssd_conv_w"][l], ssd_cb=row(p["ssd_conv_b"][l]),
        dtb=_pad_cols(row(p["ssd_dt_bias"][l]), LANES), alog=_pad_cols(row(p["ssd_a_log"][l]), LANES),
        dd=row(jnp.repeat(p["ssd_d"][l], P_SSD)), ssd_ng=row(p["ssd_norm_g"][l]),
        conf_w=p["conf_conv_w"][l], conf_b=row(p["conf_conv_b"][l]),
        conf_lg=row(p["conf_ln_g"][l]), conf_lb=row(p["conf_ln_b"][l]),
        w_out=p["w_out"][l].astype(BF16), g_ffn=row(p["norm_ffn_g"][l]),
        w_gate=w_up[:, :D_FF].astype(BF16), w_val=w_up[:, D_FF:].astype(BF16),
        ffn_cw=p["ffn_conv_w"][l], ffn_cb=row(p["ffn_conv_b"][l]),
        w_down=p["w_down"][l].astype(BF16),
    )


def _xbc_cols(proj3):
    return jnp.concatenate([proj3[..., COL_XS:COL_XS + D_SSD], proj3[..., COL_BC:COL_BC + 2 * G_SSD * N_SSD]],
                           axis=-1)


def kernel(x_prompt, x_sample, cache_k, cache_v, state_ssm, state_ssd_conv, state_conf_conv, state_ffn_conv, page_table, norm_mix_g, w_in, att_logit_bias, att_norm_g, ssd_conv_w, ssd_conv_b, ssd_dt_bias, ssd_a_log, ssd_d, ssd_norm_g, conf_conv_w, conf_conv_b, conf_ln_g, conf_ln_b, w_out, norm_ffn_g, w_up, ffn_conv_w, ffn_conv_b, w_down, norm_final_g):
    params = dict(norm_mix_g=norm_mix_g, w_in=w_in, att_logit_bias=att_logit_bias, att_norm_g=att_norm_g,
                  ssd_conv_w=ssd_conv_w, ssd_conv_b=ssd_conv_b, ssd_dt_bias=ssd_dt_bias, ssd_a_log=ssd_a_log,
                  ssd_d=ssd_d, ssd_norm_g=ssd_norm_g, conf_conv_w=conf_conv_w, conf_conv_b=conf_conv_b,
                  conf_ln_g=conf_ln_g, conf_ln_b=conf_ln_b, w_out=w_out, norm_ffn_g=norm_ffn_g, w_up=w_up,
                  ffn_conv_w=ffn_conv_w, ffn_conv_b=ffn_conv_b, w_down=w_down)
    depth = w_in.shape[0]
    bp, sp, d = x_prompt.shape
    bs, ss, _ = x_sample.shape
    assert ss == SUBLANES and d == D_MODEL
    n_pool = cache_k.shape[1]
    xp = x_prompt.reshape(bp * sp, d)
    xs = x_sample.reshape(bs * ss, d)
    uo = _cumsum_matrix()
    tri = jnp.tril(jnp.ones((SSD_CHUNK, SSD_CHUNK), F32)).astype(BF16)
    zeros_ssd_conv = jnp.zeros((bp, SUBLANES, SSD_CONV_DIM), F32)
    zeros_conf_conv = jnp.zeros((bp, CONF_PAD, C_CONF), F32)
    outs_p, outs_s = [], []
    for l in range(depth):
        w = _layer_weights(l, params)

        proj, dtp = _in_proj(xp, w["g_mix"], w["w_main"], w["w_dt"])
        att, k_t, v_t = _att_prompt(proj, w["att_bias"], w["att_g"], uo, bp, sp)
        y, h_t = _ssd(proj, dtp, zeros_ssd_conv, None, w["ssd_cw"], w["ssd_cb"], w["dtb"], w["alog"], w["dd"],
                      w["ssd_ng"], tri, bp, sp, BF16)
        u, conf_tail = _conformer(proj, zeros_conf_conv, w["conf_w"], w["conf_b"], w["conf_lg"], w["conf_lb"],
                                  bp, sp, BF16)
        x1, h2 = _out_proj(xp, att, y, u, w["w_out"], w["g_ffn"])
        act, gate_tail = _ffn_up(h2, w["w_gate"], w["w_val"], w["ffn_cw"], w["ffn_cb"], sp)
        xp = _ffn_down(x1, act, w["w_down"])
        proj3 = proj.reshape(bp, sp, D_PROJ)
        tiles_per_seq = gate_tail.shape[0] // SUBLANES // bp
        gt = gate_tail.reshape(bp, tiles_per_seq, SUBLANES, D_FF)
        outs_p.append((
            k_t, v_t,
            h_t,
            _xbc_cols(proj3[:, sp - (SSD_CONV_W - 1):]),
            conf_tail[:, CONF_PAD - (CONF_W - 1):],
            gt[:, -1, SUBLANES - (FFN_CONV_W - 1):]))

        proj, dtp = _in_proj(xs, w["g_mix"], w["w_main"], w["w_dt"])
        bias_rows = jnp.broadcast_to(jnp.repeat(w["att_bias"], SUBLANES)[:, None], (H_ATT * SUBLANES, TK_ATT))
        att = _att_sample(proj, cache_k, cache_v, page_table, l, n_pool, bias_rows, w["att_g"], uo)
        conv0 = jnp.pad(state_ssd_conv[l], ((0, 0), (SUBLANES - (SSD_CONV_W - 1), 0), (0, 0)))
        y, h_t = _ssd(proj, dtp, conv0, state_ssm[l], w["ssd_cw"], w["ssd_cb"], w["dtb"],
                      w["alog"], w["dd"], w["ssd_ng"], tri, bs, ss, F32)
        conf0 = jnp.pad(state_conf_conv[l], ((0, 0), (CONF_PAD - (CONF_W - 1), 0), (0, 0)))
        u, conf_tail = _conformer(proj, conf0, w["conf_w"], w["conf_b"], w["conf_lg"], w["conf_lb"], bs, ss, F32)
        x1, h2 = _out_proj(xs, att, y, u, w["w_out"], w["g_ffn"])
        st = state_ffn_conv[l]
        s1 = jnp.concatenate([st[:, 1:2], jnp.zeros((bs, ss - 1, D_FF), F32)], axis=1).reshape(bs * ss, -1)
        s2 = jnp.concatenate([st, jnp.zeros((bs, ss - 2, D_FF), F32)], axis=1).reshape(bs * ss, -1)
        act, gate_full = _ffn_up(h2, w["w_gate"], w["w_val"], w["ffn_cw"], w["ffn_cb"], ss, s1, s2)
        xs = _ffn_down(x1, act, w["w_down"])
        proj3 = proj.reshape(bs, ss, D_PROJ)
        outs_s.append((
            proj3[..., COL_K:COL_K + D_ATT].reshape(bs, ss, H_ATT, D_HEAD_ATT),
            proj3[..., COL_V:COL_V + D_ATT].reshape(bs, ss, H_ATT, D_HEAD_ATT),
            h_t,
            _xbc_cols(proj3[:, ss - (SSD_CONV_W - 1):]),
            conf_tail[:, CONF_PAD - (CONF_W - 1):],
            gate_full.reshape(bs, ss, D_FF)[:, ss - (FFN_CONV_W - 1):]))

    g_final = norm_final_g.reshape(1, d)
    y_prompt = _final_norm(xp, g_final).reshape(bp, sp, d)
    y_sample = _final_norm(xs, g_final).reshape(bs, ss, d)
    stack = lambda outs, i: jnp.stack([o[i] for o in outs])

    def kv_prompt(i):
        t = stack(outs_p, i).reshape(depth, bp, H_ATT, D_HEAD_ATT, sp)
        return jnp.transpose(t, (0, 1, 4, 2, 3))

    return (y_prompt, y_sample,
            kv_prompt(0), kv_prompt(1), stack(outs_s, 0), stack(outs_s, 1),
            stack(outs_p, 2), stack(outs_s, 2), stack(outs_p, 3), stack(outs_s, 3),
            stack(outs_p, 4), stack(outs_s, 4), stack(outs_p, 5), stack(outs_s, 5))
```

```python
import functools
import math

import jax
import jax.numpy as jnp
from jax import lax
from jax.experimental import pallas as pl
from jax.experimental.pallas import tpu as pltpu

F32 = jnp.float32
BF16 = jnp.bfloat16
SDS = jax.ShapeDtypeStruct

D_MODEL = 2048
D_HEAD_ATT = 64
D_ATT = 512
H_ATT = 8
D_SSD = 1024
P_SSD = 64
H_SSD = 16
G_SSD = 2
N_SSD = 128
SSD_CONV_W = 4
SSD_CONV_DIM = D_SSD + 2 * G_SSD * N_SSD
SSD_CHUNK = 128
C_CONF = 512
CONF_W = 31
D_FF = 5504
FFN_CONV_W = 3
PAGE_SIZE = 128
EPS = 1e-6

LANES = 128
SUBLANES = 8
VMEM_BYTES_V7X = 64 * 1024 * 1024
VMEM_LIMIT_MAX = VMEM_BYTES_V7X - 8 * 1024 * 1024
VMEM_COMPILER_SCRATCH = 16 * 1024 * 1024

D_PROJ = 5120
COL_Z, COL_XS, COL_GLU, COL_BC, COL_Q, COL_K, COL_V = 0, 1024, 2048, 3072, 3584, 4096, 4608
FFN_TN = 512
CONF_PAD = 32
TK_ATT = 128


def _nbytes(shape, dtype):
    return math.prod(shape) * jnp.dtype(dtype).itemsize


def _cparams(sem, bufs):
    est = sum(_nbytes(s, d) * n for s, d, n in bufs)
    return pltpu.CompilerParams(dimension_semantics=sem,
                                vmem_limit_bytes=int(min(est + VMEM_COMPILER_SCRATCH, VMEM_LIMIT_MAX)))


def _tile(m, pref):
    t = min(m, pref)
    assert m % t == 0, (m, pref)
    return t


def _sigmoid(x):
    return 0.5 * jnp.tanh(0.5 * x) + 0.5


def _softplus(x):
    return jnp.maximum(x, 0.0) + jnp.log1p(jnp.exp(-jnp.abs(x)))


def _split3(x):
    a = x.astype(BF16)
    r = x - a.astype(F32)
    b = r.astype(BF16)
    c = (r - b.astype(F32)).astype(BF16)
    return a, b, c


def _inproj_body(x_ref, g_ref, w_ref, wdt_ref, o_ref, dt_ref, hn_ref):
    @pl.when(pl.program_id(1) == 0)
    def _():
        x = x_ref[...]
        ms = jnp.mean(x * x, axis=-1, keepdims=True)
        hn_ref[...] = ((x * lax.rsqrt(ms + EPS)) * g_ref[...]).astype(BF16)
        dt_ref[...] = jnp.dot(hn_ref[...], wdt_ref[...], preferred_element_type=F32)

    o_ref[...] = jnp.dot(hn_ref[...], w_ref[...], preferred_element_type=F32)


def _in_proj(x, g, w, wdt):
    m, d = x.shape
    n = w.shape[1]
    tm, tn = _tile(m, 1024), _tile(n, 1024)
    return pl.pallas_call(
        _inproj_body,
        grid=(m // tm, n // tn),
        in_specs=[pl.BlockSpec((tm, d), lambda i, j: (i, 0)),
                  pl.BlockSpec((1, d), lambda i, j: (0, 0)),
                  pl.BlockSpec((d, tn), lambda i, j: (0, j)),
                  pl.BlockSpec((d, LANES), lambda i, j: (0, 0))],
        out_specs=[pl.BlockSpec((tm, tn), lambda i, j: (i, j)),
                   pl.BlockSpec((tm, LANES), lambda i, j: (i, 0))],
        out_shape=[SDS((m, n), F32), SDS((m, LANES), F32)],
        scratch_shapes=[pltpu.VMEM((tm, d), BF16)],
        compiler_params=_cparams(("parallel", "arbitrary"),
                                 [((tm, d), F32, 3), ((tm, d), BF16, 1), ((d, tn), BF16, 2),
                                  ((tm, tn), F32, 3), ((tm, LANES), F32, 2), ((d, LANES), BF16, 2)]),
        name="in_proj",
    )(x, g, w, wdt)


_NT = (((1,), (1,)), ((), ()))
LOG2E = 1.4426950408889634


def _sb_logs(z):
    neg_abs = lax.bitcast_convert_type(lax.bitcast_convert_type(z, jnp.uint32) | jnp.uint32(0x80000000), F32)
    lbeta = jnp.minimum(z, 0.0) - jnp.log2(1.0 + jnp.exp2(neg_abs))
    return lbeta, lbeta - z


def _sb_update(qm, kblk, vblk, bias, uo, carry, acc, mask):
    z = lax.dot_general(qm, kblk, _NT, preferred_element_type=F32) + bias
    lbeta, lk = _sb_logs(z)
    lk = jnp.where(mask, lk, 0.0)
    r = jnp.dot(lk.astype(BF16), uo, preferred_element_type=F32)
    w = jnp.where(mask, jnp.exp2(lbeta + r[:, :TK_ATT] + carry), 0.0)
    acc = acc + jnp.dot(w.astype(BF16), vblk, preferred_element_type=F32)
    return carry + r[:, TK_ATT:], acc


def _add_from_row(x, r0, delta):
    if r0 == 0:
        return x + delta
    return jnp.concatenate([x[:r0, :], x[r0:, :] + delta], axis=0)


def _cumsum_matrix():
    jp = lax.broadcasted_iota(jnp.int32, (TK_ATT, 2 * TK_ATT), 0)
    j = lax.broadcasted_iota(jnp.int32, (TK_ATT, 2 * TK_ATT), 1)
    return jnp.where((jp > j) | (j >= TK_ATT), 1.0, 0.0).astype(BF16)


def _att_prompt_body(bias_ref, q_ref, k_ref, v_ref, g_ref, uo_ref, o_ref, kt_ref, vt_ref, carry_ref, acc_ref,
                     *, tq, scale):
    hp = pl.program_id(1)
    qi = pl.program_id(2)

    @pl.when(qi == 0)
    def _():
        for blk in range(k_ref.shape[0] // LANES):
            rows = slice(blk * LANES, (blk + 1) * LANES)
            kt_ref[0, :, rows] = k_ref[rows, :].T
            vt_ref[0, :, rows] = v_ref[rows, :].T

    nsub = tq // TK_ATT
    lane = lax.broadcasted_iota(jnp.int32, (1, LANES), 1)
    low = lane < D_HEAD_ATT
    uo = uo_ref[...]
    qb = (q_ref[...] * (scale * LOG2E)).astype(BF16)
    lane2 = lax.broadcasted_iota(jnp.int32, (1, 2 * TK_ATT), 1)
    bias = jnp.where(lane2 < TK_ATT, bias_ref[2 * hp], bias_ref[2 * hp + 1]) * LOG2E
    qrow = lax.broadcasted_iota(jnp.int32, (tq, 1), 0)
    carry_ref[...] = jnp.zeros_like(carry_ref)
    acc_ref[...] = jnp.zeros_like(acc_ref)

    def span_step(ks, nblk, masked):
        k0 = pl.multiple_of(ks * tq, tq)
        carries = [carry_ref[0], carry_ref[1]]
        accs = [acc_ref[0], acc_ref[1]]
        order = list(reversed(range(nblk)))
        zs, vblks, masks, row0s = [], [], [], []
        for sb in order:
            rows = pl.ds(k0 + sb * TK_ATT, TK_ATT)
            r0 = sb * TK_ATT if masked else 0
            kf = k_ref[rows, :]
            kbd = jnp.concatenate([jnp.where(low, kf, 0.0), jnp.where(low, 0.0, kf)], axis=0).astype(BF16)
            vblks.append(v_ref[rows, :].astype(BF16))
            zs.append(lax.dot_general(qb[r0:, :], kbd, _NT, preferred_element_type=F32) + bias)
            masks.append((sb * TK_ATT + (lane2 & (TK_ATT - 1)) < qrow[r0:, :]) if masked else None)
            row0s.append(r0)
        lbetas, rs = [], []
        for z, mask in zip(zs, masks):
            lbeta, lk = _sb_logs(z)
            if masked:
                lk = jnp.where(mask, lk, 0.0)
            lkb = lk.astype(BF16)
            lbetas.append(lbeta)
            rs.append([jnp.dot(lkb[:, hh * TK_ATT:(hh + 1) * TK_ATT], uo, preferred_element_type=F32)
                       for hh in range(2)])
        for lbeta, r2, mask, vblk, r0 in zip(lbetas, rs, masks, vblks, row0s):
            for hh in range(2):
                hs = slice(hh * TK_ATT, (hh + 1) * TK_ATT)
                w = jnp.exp2(lbeta[:, hs] + r2[hh][:, :TK_ATT] + carries[hh][r0:, :])
                if masked:
                    w = jnp.where(mask[:, hs], w, 0.0)
                pv = jnp.dot(w.astype(BF16), vblk, preferred_element_type=F32)
                accs[hh] = _add_from_row(accs[hh], r0, pv)
                carries[hh] = _add_from_row(carries[hh], r0, r2[hh][:, TK_ATT:])
        for hh in range(2):
            carry_ref[hh] = carries[hh]
            acc_ref[hh] = accs[hh]

    span_step(qi, nsub, True)

    def off_diag(t, _):
        span_step(qi - 1 - t, nsub, False)
        return 0

    lax.fori_loop(0, qi, off_diag, 0)
    o = jnp.where(low, acc_ref[0], acc_ref[1])
    sq = o * o
    s0 = jnp.sum(jnp.where(low, sq, 0.0), axis=-1, keepdims=True)
    s1 = jnp.sum(jnp.where(low, 0.0, sq), axis=-1, keepdims=True)
    ms = jnp.where(low, s0, s1) * (1.0 / D_HEAD_ATT)
    o_ref[...] = ((o * lax.rsqrt(ms + EPS)) * g_ref[...]).astype(o_ref.dtype)


def _att_prompt(proj, bias, g, uo, nb, seq):
    m = proj.shape[0]
    tq = _tile(seq, 512)
    nq = seq // tq
    npair = H_ATT // 2
    body = functools.partial(_att_prompt_body, tq=tq, scale=D_HEAD_ATT ** -0.5)
    return pl.pallas_call(
        body,
        grid=(nb, npair, nq),
        in_specs=[pl.BlockSpec(memory_space=pltpu.SMEM),
                  pl.BlockSpec((tq, LANES), lambda b, h, i: (b * nq + i, COL_Q // LANES + h)),
                  pl.BlockSpec((seq, LANES), lambda b, h, i: (b, COL_K // LANES + h)),
                  pl.BlockSpec((seq, LANES), lambda b, h, i: (b, COL_V // LANES + h)),
                  pl.BlockSpec((1, LANES), lambda b, h, i: (0, h)),
                  pl.BlockSpec((TK_ATT, 2 * TK_ATT), lambda b, h, i: (0, 0))],
        out_specs=[pl.BlockSpec((tq, LANES), lambda b, h, i: (b * nq + i, h)),
                   pl.BlockSpec((1, LANES, seq), lambda b, h, i: (b, h, 0)),
                   pl.BlockSpec((1, LANES, seq), lambda b, h, i: (b, h, 0))],
        out_shape=[SDS((m, D_ATT), BF16), SDS((nb, D_ATT, seq), F32), SDS((nb, D_ATT, seq), F32)],
        scratch_shapes=[pltpu.VMEM((2, tq, TK_ATT), F32), pltpu.VMEM((2, tq, LANES), F32)],
        compiler_params=_cparams(("parallel", "parallel", "arbitrary"),
                                 [((seq, LANES), F32, 8), ((tq, LANES), F32, 24)]),
        name="att_prompt",
    )(bias, proj, proj, proj, g, uo)


def _att_sample_body(pt_ref, q_ref, kn_ref, vn_ref, bias_ref, g_ref, uo_ref, *rest, npg, scale):
    k_refs, v_refs = rest[:npg], rest[npg:2 * npg]
    o_ref, qbd_ref, kpad_ref, vpad_ref, carry_ref, acc_ref = rest[2 * npg:]
    del pt_ref
    p = pl.program_id(1)
    nrow = H_ATT * SUBLANES
    row = lax.broadcasted_iota(jnp.int32, (nrow, D_ATT), 0)
    lane = lax.broadcasted_iota(jnp.int32, (nrow, D_ATT), 1)
    own = (row >> 3) == (lane >> 6)
    uo = uo_ref[...]
    bias = bias_ref[...] * LOG2E

    @pl.when(p == 0)
    def _():
        q = q_ref[...] * (scale * LOG2E)
        qt = jnp.concatenate([q] * H_ATT, axis=0)
        qbd = jnp.where(own, qt, 0.0).astype(BF16)
        qbd_ref[...] = qbd
        kpad_ref[...] = jnp.zeros_like(kpad_ref)
        vpad_ref[...] = jnp.zeros_like(vpad_ref)
        kpad_ref[0:SUBLANES, :] = kn_ref[...]
        vpad_ref[0:SUBLANES, :] = vn_ref[...]
        tok = lax.broadcasted_iota(jnp.int32, (nrow, 1), 0) & (SUBLANES - 1)
        mask = lax.broadcasted_iota(jnp.int32, (1, TK_ATT), 1) < tok
        carry, acc = _sb_update(qbd, kpad_ref[...].astype(BF16), vpad_ref[...].astype(BF16), bias, uo,
                                jnp.zeros((nrow, TK_ATT), F32), jnp.zeros((nrow, D_ATT), F32), mask)
        carry_ref[...] = carry
        acc_ref[...] = acc

    carry, acc = carry_ref[...], acc_ref[...]
    qbd = qbd_ref[...]
    zs = [jnp.dot(qbd, k_refs[gi][0].astype(BF16), preferred_element_type=F32) + bias for gi in range(npg)]
    lbetas, rs = [], []
    for z in zs:
        lbeta, lk = _sb_logs(z)
        lbetas.append(lbeta)
        rs.append(jnp.dot(lk.astype(BF16), uo, preferred_element_type=F32))
    for gi in range(npg):
        w = jnp.exp2(lbetas[gi] + rs[gi][:, :TK_ATT] + carry)
        acc = acc + lax.dot_general(w.astype(BF16), v_refs[gi][0].astype(BF16), _NT, preferred_element_type=F32)
        carry = carry + rs[gi][:, TK_ATT:]
    carry_ref[...] = carry
    acc_ref[...] = acc

    @pl.when(p == pl.num_programs(1) - 1)
    def _():
        a = jnp.where(own, acc, 0.0)
        ms = jnp.sum(a * a, axis=-1, keepdims=True) * (1.0 / D_HEAD_ATT)
        a = a * lax.rsqrt(ms + EPS)
        o = a[0:SUBLANES]
        for h in range(1, H_ATT):
            o = o + a[h * SUBLANES:(h + 1) * SUBLANES]
        o_ref[...] = o * g_ref[...]


def _att_sample(proj, cache_k, cache_v, page_table, layer, n_pool, bias_rows, g, uo):
    m = proj.shape[0]
    nseq, npages = page_table.shape
    assert m == nseq * SUBLANES
    npg = next(n for n in (32, 16, 8, 4, 2, 1) if npages % n == 0)
    pt = page_table.reshape(-1)
    ck = jnp.transpose(cache_k, (0, 1, 3, 4, 2)).reshape(-1, D_ATT, PAGE_SIZE)
    cv = jnp.transpose(cache_v, (0, 1, 3, 4, 2)).reshape(-1, D_ATT, PAGE_SIZE)
    nrow = H_ATT * SUBLANES

    def page_map(gi):
        def f(b, p, pt_ref):
            return (layer * n_pool + pt_ref[b * npages + (npages - 1 - (p * npg + gi))], 0, 0)
        return f

    page_specs = [pl.BlockSpec((1, D_ATT, PAGE_SIZE), page_map(gi)) for gi in range(npg)]
    body = functools.partial(_att_sample_body, npg=npg, scale=D_HEAD_ATT ** -0.5)
    grid_spec = pltpu.PrefetchScalarGridSpec(
        num_scalar_prefetch=1,
        grid=(nseq, npages // npg),
        in_specs=[pl.BlockSpec((SUBLANES, D_ATT), lambda b, p, pt_ref: (b, COL_Q // D_ATT)),
                  pl.BlockSpec((SUBLANES, D_ATT), lambda b, p, pt_ref: (b, COL_K // D_ATT)),
                  pl.BlockSpec((SUBLANES, D_ATT), lambda b, p, pt_ref: (b, COL_V // D_ATT)),
                  pl.BlockSpec((nrow, TK_ATT), lambda b, p, pt_ref: (0, 0)),
                  pl.BlockSpec((1, D_ATT), lambda b, p, pt_ref: (0, 0)),
                  pl.BlockSpec((TK_ATT, 2 * TK_ATT), lambda b, p, pt_ref: (0, 0))] + page_specs + page_specs,
        out_specs=pl.BlockSpec((SUBLANES, D_ATT), lambda b, p, pt_ref: (b, 0)),
        scratch_shapes=[pltpu.VMEM((nrow, D_ATT), BF16),
                        pltpu.VMEM((TK_ATT, D_ATT), F32),
                        pltpu.VMEM((TK_ATT, D_ATT), F32),
                        pltpu.VMEM((nrow, TK_ATT), F32),
                        pltpu.VMEM((nrow, D_ATT), F32)])
    return pl.pallas_call(
        body,
        grid_spec=grid_spec,
        out_shape=SDS((m, D_ATT), F32),
        compiler_params=_cparams(("parallel", "arbitrary"),
                                 [((PAGE_SIZE, D_ATT), F32, 4 * npg), ((TK_ATT, D_ATT), F32, 4)]),
        name="att_sample",
    )(pt, proj, proj, proj, bias_rows, g, uo, *([ck] * npg), *([cv] * npg))


def _ssd_body(*refs, lr, nc, has_h0):
    if has_h0:
        (z_ref, xs_ref, bc_ref, dt_ref, c0_ref, h0_ref, cw_ref, cb_ref, dtb_ref, alog_ref, dd_ref, ng_ref,
         tri_ref, y_ref, ht_ref, win_ref, hs_ref, dts_ref, ysc_ref, xd_ref, et_ref) = refs
    else:
        (z_ref, xs_ref, bc_ref, dt_ref, c0_ref, cw_ref, cb_ref, dtb_ref, alog_ref, dd_ref, ng_ref,
         tri_ref, y_ref, ht_ref, win_ref, hs_ref, dts_ref, ysc_ref, xd_ref, et_ref) = refs
        h0_ref = None
    L = SSD_CHUNK
    c = pl.program_id(1)
    hist = SUBLANES
    pairs_per_group = H_SSD // G_SSD // 2

    @pl.when(c == 0)
    def _():
        win_ref[0:hist, :] = c0_ref[0]
        if lr < L:
            win_ref[hist:hist + L, :] = jnp.zeros((L, SSD_CONV_DIM), F32)
            dts_ref[...] = jnp.zeros_like(dts_ref)
        if has_h0:
            for hp in range(H_SSD // 2):
                pair = jnp.concatenate([h0_ref[0, 2 * hp], h0_ref[0, 2 * hp + 1]], axis=0)
                gsl = slice((hp % pairs_per_group) * LANES, (hp % pairs_per_group + 1) * LANES)
                hs_ref[hp // pairs_per_group, :, gsl] = pair.T
        else:
            hs_ref[...] = jnp.zeros_like(hs_ref)

    win_ref[hist:hist + lr, 0:D_SSD] = xs_ref[...]
    win_ref[hist:hist + lr, D_SSD:SSD_CONV_DIM] = bc_ref[...]
    dts_ref[0:lr, :] = dt_ref[...]

    lane = lax.broadcasted_iota(jnp.int32, (1, LANES), 1)
    low = lane < P_SSD
    rowi = lax.broadcasted_iota(jnp.int32, (L, 1), 0)
    dt = _softplus(dts_ref[...] + dtb_ref[...])
    if lr < L:
        dt = jnp.where(rowi < lr, dt, 0.0)
    a_head = jnp.where(lane < H_SSD, -jnp.exp(alog_ref[...]), 0.0)
    a = dt * a_head
    tri = tri_ref[...]
    a1, a2, a3 = _split3(a)
    a_cs = (jnp.dot(tri, a1, preferred_element_type=F32) + jnp.dot(tri, a2, preferred_element_type=F32)
            + jnp.dot(tri, a3, preferred_element_type=F32))
    a_cs_t = a_cs.T
    a_tot = a_cs[L - 1:L, :]
    causal = rowi >= lax.broadcasted_iota(jnp.int32, (1, L), 1)

    pre = cb_ref[...]
    for k in range(SSD_CONV_W):
        off = hist - (SSD_CONV_W - 1) + k
        pre = pre + cw_ref[k:k + 1, :] * win_ref[off:off + L, :]
    xbc = pre * _sigmoid(pre)
    win_ref[0:hist, :] = win_ref[L:L + hist, :]

    cbs, bts, cgs = [], [], []
    for g in range(G_SSD):
        bg = xbc[:, D_SSD + g * N_SSD:D_SSD + (g + 1) * N_SSD]
        cg = xbc[:, D_SSD + (G_SSD + g) * N_SSD:D_SSD + (G_SSD + g + 1) * N_SSD].astype(BF16)
        cbs.append(lax.dot_general(cg, bg.astype(BF16), (((1,), (1,)), ((), ())), preferred_element_type=F32))
        bts.append(bg.T.astype(BF16))
        cgs.append(cg)

    for hp in range(H_SSD // 2):
        g = hp // pairs_per_group
        cols, dtcols, mats = [], [], []
        for h in (2 * hp, 2 * hp + 1):
            col = jnp.broadcast_to(a_cs[:, h:h + 1], (L, LANES))
            dec = jnp.exp(jnp.where(causal, col - a_cs_t[h:h + 1, :], -1e30))
            mats.append((cbs[g] * dec).astype(BF16))
            cols.append(col)
            dtcols.append(jnp.broadcast_to(dt[:, h:h + 1], (L, LANES)))
        sl = slice(hp * LANES, (hp + 1) * LANES)
        xs_pair = xbc[:, sl]
        xdt = xs_pair * jnp.where(low, dtcols[0], dtcols[1])
        xdt_bf = xdt.astype(BF16)
        ydiag = jnp.where(low, jnp.dot(mats[0], xdt_bf, preferred_element_type=F32),
                          jnp.dot(mats[1], xdt_bf, preferred_element_type=F32))
        cs_pair = jnp.where(low, cols[0], cols[1])
        gsl = slice((hp % pairs_per_group) * LANES, (hp % pairs_per_group + 1) * LANES)
        yoff = jnp.dot(cgs[g], hs_ref[g, :, gsl].astype(BF16), preferred_element_type=F32) * jnp.exp(cs_pair)
        ysc_ref[:, sl] = ydiag + yoff + dd_ref[:, sl] * xs_pair
        tot_pair = jnp.where(low, jnp.broadcast_to(a_tot[:, 2 * hp:2 * hp + 1], (1, LANES)),
                             jnp.broadcast_to(a_tot[:, 2 * hp + 1:2 * hp + 2], (1, LANES)))
        xd_ref[:, sl] = (xdt * jnp.exp(tot_pair - cs_pair)).astype(BF16)
        et_ref[:, sl] = jnp.exp(tot_pair)

    gw = D_SSD // G_SSD
    for g in range(G_SSD):
        gs = slice(g * gw, (g + 1) * gw)
        cst = jnp.dot(bts[g], xd_ref[:, gs], preferred_element_type=F32)
        hs_ref[g] = hs_ref[g] * et_ref[:, gs] + cst

    zz = z_ref[...]
    y = ysc_ref[0:lr, :] * (zz * _sigmoid(zz))
    for g in range(G_SSD):
        gs = slice(g * gw, (g + 1) * gw)
        yg = y[:, gs]
        ms = jnp.mean(yg * yg, axis=-1, keepdims=True)
        y_ref[:, gs] = ((yg * lax.rsqrt(ms + EPS)) * ng_ref[:, gs]).astype(y_ref.dtype)

    @pl.when(c == nc - 1)
    def _():
        for hp in range(H_SSD // 2):
            gsl = slice((hp % pairs_per_group) * LANES, (hp % pairs_per_group + 1) * LANES)
            pair_t = hs_ref[hp // pairs_per_group, :, gsl].T
            ht_ref[0, 2 * hp] = pair_t[0:P_SSD]
            ht_ref[0, 2 * hp + 1] = pair_t[P_SSD:2 * P_SSD]


def _ssd(proj, dtp, conv0, h0, cw, cb, dtb, alog, dd, ng, tri, nb, seq, out_dtype):
    m = proj.shape[0]
    lr = min(seq, SSD_CHUNK)
    assert seq % lr == 0
    nc = seq // lr
    has_h0 = h0 is not None
    gw = D_SSD // G_SSD
    row = lambda b, c: b * nc + c
    in_specs = [pl.BlockSpec((lr, D_SSD), lambda b, c: (row(b, c), COL_Z // D_SSD)),
                pl.BlockSpec((lr, D_SSD), lambda b, c: (row(b, c), COL_XS // D_SSD)),
                pl.BlockSpec((lr, 2 * G_SSD * N_SSD), lambda b, c: (row(b, c), COL_BC // (2 * G_SSD * N_SSD))),
                pl.BlockSpec((lr, LANES), lambda b, c: (row(b, c), 0)),
                pl.BlockSpec((1, SUBLANES, SSD_CONV_DIM), lambda b, c: (b, 0, 0))]
    args = [proj, proj, proj, dtp, conv0]
    if has_h0:
        in_specs.append(pl.BlockSpec((1, H_SSD, P_SSD, N_SSD), lambda b, c: (b, 0, 0, 0)))
        args.append(h0)
    const = lambda shape: pl.BlockSpec(shape, lambda b, c: (0,) * len(shape))
    in_specs += [const((SSD_CONV_W, SSD_CONV_DIM)), const((1, SSD_CONV_DIM)), const((1, LANES)),
                 const((1, LANES)), const((1, D_SSD)), const((1, D_SSD)), const((SSD_CHUNK, SSD_CHUNK))]
    args += [cw, cb, dtb, alog, dd, ng, tri]
    body = functools.partial(_ssd_body, lr=lr, nc=nc, has_h0=has_h0)
    L = SSD_CHUNK
    return pl.pallas_call(
        body,
        grid=(nb, nc),
        in_specs=in_specs,
        out_specs=[pl.BlockSpec((lr, D_SSD), lambda b, c: (row(b, c), 0)),
                   pl.BlockSpec((1, H_SSD, P_SSD, N_SSD), lambda b, c: (b, 0, 0, 0))],
        out_shape=[SDS((m, D_SSD), out_dtype), SDS((nb, H_SSD, P_SSD, N_SSD), F32)],
        scratch_shapes=[pltpu.VMEM((L + SUBLANES, SSD_CONV_DIM), F32),
                        pltpu.VMEM((G_SSD, N_SSD, gw), F32),
                        pltpu.VMEM((L, LANES), F32),
                        pltpu.VMEM((L, D_SSD), F32),
                        pltpu.VMEM((L, D_SSD), BF16),
                        pltpu.VMEM((1, D_SSD), F32)],
        compiler_params=_cparams(("parallel", "arbitrary"),
                                 [((L, SSD_CONV_DIM), F32, 12), ((G_SSD, N_SSD, gw), F32, 5)]),
        name="ssd",
    )(*args)


def _conf_body(glu_ref, c0_ref, w_ref, b_ref, lg_ref, lb_ref, o_ref, tail_ref, win_ref, part_ref, *, tt, nt):
    t = pl.program_id(1)

    @pl.when(t == 0)
    def _():
        win_ref[0:CONF_PAD, :] = c0_ref[0]

    glu = glu_ref[...]
    win_ref[CONF_PAD:CONF_PAD + tt, :] = glu[:, :C_CONF] * _sigmoid(glu[:, C_CONF:])
    first = CONF_PAD - (CONF_W - 1)
    acc = jnp.broadcast_to(b_ref[...], (tt, C_CONF))
    for r in range(SUBLANES):
        offs = [j for j in range(r, first + CONF_W, SUBLANES) if j >= first]
        rows = tt if r == 0 else tt + SUBLANES
        part = w_ref[offs[0] - first:offs[0] - first + 1, :] * win_ref[offs[0] - r:offs[0] - r + rows, :]
        for j in offs[1:]:
            part = part + w_ref[j - first:j - first + 1, :] * win_ref[j - r:j - r + rows, :]
        if r == 0:
            acc = acc + part
        else:
            part_ref[r - 1] = part
            acc = acc + part_ref[r - 1, r:r + tt, :]
    mu = jnp.mean(acc, axis=-1, keepdims=True)
    xc = acc - mu
    var = jnp.mean(xc * xc, axis=-1, keepdims=True)
    yn = (xc * lax.rsqrt(var + EPS)) * lg_ref[...] + lb_ref[...]
    o_ref[...] = (yn * _sigmoid(yn)).astype(o_ref.dtype)
    tail = win_ref[tt:tt + CONF_PAD, :]
    tail_ref[0] = tail
    if nt > 1:
        win_ref[0:CONF_PAD, :] = tail


def _conformer(proj, conv0, w, b, lg, lb, nb, seq, out_dtype):
    m = proj.shape[0]
    tt = _tile(seq, 512)
    nt = seq // tt
    assert nt == 1 or tt >= CONF_PAD
    body = functools.partial(_conf_body, tt=tt, nt=nt)
    const = lambda shape: pl.BlockSpec(shape, lambda bb, t: (0,) * len(shape))
    return pl.pallas_call(
        body,
        grid=(nb, nt),
        in_specs=[pl.BlockSpec((tt, 2 * C_CONF), lambda bb, t: (bb * nt + t, COL_GLU // (2 * C_CONF))),
                  pl.BlockSpec((1, CONF_PAD, C_CONF), lambda bb, t: (bb, 0, 0)),
                  const((CONF_W, C_CONF)), const((1, C_CONF)), const((1, C_CONF)), const((1, C_CONF))],
        out_specs=[pl.BlockSpec((tt, C_CONF), lambda bb, t: (bb * nt + t, 0)),
                   pl.BlockSpec((1, CONF_PAD, C_CONF), lambda bb, t: (bb, 0, 0))],
        out_shape=[SDS((m, C_CONF), out_dtype), SDS((nb, CONF_PAD, C_CONF), F32)],
        scratch_shapes=[pltpu.VMEM((CONF_PAD + tt, C_CONF), F32),
                        pltpu.VMEM((SUBLANES - 1, tt + SUBLANES, C_CONF), F32)],
        compiler_params=_cparams(("parallel", "arbitrary"),
                                 [((tt, 2 * C_CONF), F32, 8), ((SUBLANES, tt + SUBLANES, C_CONF), F32, 1)]),
        name="conformer",
    )(proj, conv0, w, b, lg, lb)


def _outproj_body(x_ref, a_ref, y_ref, u_ref, w_ref, g_ref, x1_ref, h_ref):
    tm = x_ref.shape[0]
    halves = [slice(0, tm // 2), slice(tm // 2, tm)] if tm % (4 * SUBLANES) == 0 else [slice(0, tm)]
    prods = []
    for rs in halves:
        p = jnp.dot(a_ref[rs, :].astype(BF16), w_ref[0:D_ATT, :], preferred_element_type=F32)
        p = p + jnp.dot(y_ref[rs, :].astype(BF16), w_ref[D_ATT:D_ATT + D_SSD, :], preferred_element_type=F32)
        p = p + jnp.dot(u_ref[rs, :].astype(BF16), w_ref[D_ATT + D_SSD:, :], preferred_element_type=F32)
        prods.append(p)
    for rs, p in zip(halves, prods):
        acc = x_ref[rs, :] + p
        x1_ref[rs, :] = acc
        ms = jnp.mean(acc * acc, axis=-1, keepdims=True)
        h_ref[rs, :] = ((acc * lax.rsqrt(ms + EPS)) * g_ref[...]).astype(BF16)


def _out_proj(x, att, y, u, w, g):
    m, d = x.shape
    tm = _tile(m, 512)
    row = lambda width: pl.BlockSpec((tm, width), lambda i: (i, 0))
    return pl.pallas_call(
        _outproj_body,
        grid=(m // tm,),
        in_specs=[row(d), row(D_ATT), row(D_SSD), row(C_CONF),
                  pl.BlockSpec((d, d), lambda i: (0, 0)), pl.BlockSpec((1, d), lambda i: (0, 0))],
        out_specs=[row(d), row(d)],
        out_shape=[SDS((m, d), F32), SDS((m, d), BF16)],
        compiler_params=_cparams(("parallel",), [((tm, d), F32, 7), ((d, d), BF16, 2)]),
        name="out_proj",
    )(x, att, y, u, w, g)


HALO = 16


def _ffn_up_body(*refs, tm, seq, sample):
    if sample:
        h_ref, wg_ref, wv_ref, cw_ref, cb_ref, s1_ref, s2_ref, act_ref, tail_ref, lhs_ref = refs
    else:
        h_ref, halo_ref, wg_ref, wv_ref, cw_ref, cb_ref, act_ref, tail_ref, lhs_ref = refs
    i = pl.program_id(0)

    @pl.when(pl.program_id(1) == 0)
    def _():
        lhs_ref[HALO:, :] = h_ref[...]
        if sample:
            lhs_ref[0:HALO, :] = jnp.zeros((HALO, D_MODEL), BF16)
        else:
            lhs_ref[0:HALO, :] = halo_ref[...]

            @pl.when((i * tm) % seq == 0)
            def _():
                lhs_ref[0:HALO, :] = jnp.zeros((HALO, D_MODEL), BF16)

    gate = jnp.dot(lhs_ref[...], wg_ref[...], preferred_element_type=F32)
    val = jnp.dot(lhs_ref[HALO:, :], wv_ref[...], preferred_element_type=F32)
    g0 = gate[HALO:HALO + tm, :]
    g1 = gate[HALO - 1:HALO - 1 + tm, :]
    g2 = gate[HALO - 2:HALO - 2 + tm, :]
    if sample:
        tok = lax.broadcasted_iota(jnp.int32, (tm, 1), 0) & (seq - 1)
        g1 = jnp.where(tok < 1, s1_ref[...], g1)
        g2 = jnp.where(tok < 2, s2_ref[...], g2)
        tail_ref[...] = g0
    else:
        tail_ref[...] = g0[tm - SUBLANES:, :]
    pre = cw_ref[0:1, :] * g2 + cw_ref[1:2, :] * g1 + cw_ref[2:3, :] * g0 + cb_ref[...]
    act_ref[...] = ((pre * _sigmoid(pre)) * val).astype(BF16)


def _ffn_up(h, wg, wv, cw, cb, seq, s1=None, s2=None):
    m, d = h.shape
    f = wg.shape[1]
    sample = s1 is not None
    tm, tn = _tile(m, m if sample else min(1024, seq)), FFN_TN
    if sample:
        assert seq & (seq - 1) == 0 and seq >= FFN_CONV_W - 1
    else:
        assert seq % tm == 0
    body = functools.partial(_ffn_up_body, tm=tm, seq=seq, sample=sample)
    hspec = pl.BlockSpec((tm, d), lambda i, j: (i, 0))
    wspec = pl.BlockSpec((d, tn), lambda i, j: (0, j))
    cspec = lambda r: pl.BlockSpec((r, tn), lambda i, j: (0, j))
    tile = pl.BlockSpec((tm, tn), lambda i, j: (i, j))
    if sample:
        in_specs = [hspec, wspec, wspec, cspec(FFN_CONV_W), cspec(1), tile, tile]
        args = (h, wg, wv, cw, cb, s1, s2)
        tail_rows, tail_spec = m, tile
    else:
        halo = pl.BlockSpec((HALO, d), lambda i, j: (jnp.maximum(i * (tm // HALO) - 1, 0), 0))
        in_specs = [hspec, halo, wspec, wspec, cspec(FFN_CONV_W), cspec(1)]
        args = (h, h, wg, wv, cw, cb)
        tail_rows, tail_spec = (m // tm) * SUBLANES, pl.BlockSpec((SUBLANES, tn), lambda i, j: (i, j))
    return pl.pallas_call(
        body,
        grid=(m // tm, pl.cdiv(f, tn)),
        in_specs=in_specs,
        out_specs=[tile, tail_spec],
        out_shape=[SDS((m, f), BF16), SDS((tail_rows, f), F32)],
        scratch_shapes=[pltpu.VMEM((HALO + tm, d), BF16)],
        compiler_params=_cparams(("parallel", "arbitrary"),
                                 [((tm, d), BF16, 3), ((d, tn), BF16, 4), ((tm, tn), F32, 8)]),
        name="ffn_up",
    )(*args)


def _ffn_down_body(x_ref, a_ref, w_ref, o_ref):
    o_ref[...] = x_ref[...] + jnp.dot(a_ref[...], w_ref[...], preferred_element_type=F32)


def _ffn_down(x, act, w):
    m, d = x.shape
    f = act.shape[1]
    tm, tn = _tile(m, 1024), _tile(d, 512)
    return pl.pallas_call(
        _ffn_down_body,
        grid=(m // tm, d // tn),
        in_specs=[pl.BlockSpec((tm, tn), lambda i, j: (i, j)),
                  pl.BlockSpec((tm, f), lambda i, j: (i, 0)),
                  pl.BlockSpec((f, tn), lambda i, j: (0, j))],
        out_specs=pl.BlockSpec((tm, tn), lambda i, j: (i, j)),
        out_shape=SDS((m, d), F32),
        compiler_params=_cparams(("parallel", "arbitrary"),
                                 [((tm, f), BF16, 2), ((f, tn), BF16, 2), ((tm, tn), F32, 6)]),
        name="ffn_down",
    )(x, act, w)


def _final_norm_body(x_ref, g_ref, o_ref):
    x = x_ref[...]
    ms = jnp.mean(x * x, axis=-1, keepdims=True)
    o_ref[...] = (x * lax.rsqrt(ms + EPS)) * g_ref[...]


def _final_norm(x, g):
    m, d = x.shape
    tm = _tile(m, 512)
    return pl.pallas_call(
        _final_norm_body,
        grid=(m // tm,),
        in_specs=[pl.BlockSpec((tm, d), lambda i: (i, 0)), pl.BlockSpec((1, d), lambda i: (0, 0))],
        out_specs=pl.BlockSpec((tm, d), lambda i: (i, 0)),
        out_shape=SDS((m, d), F32),
        compiler_params=_cparams(("parallel",), [((tm, d), F32, 6)]),
        name="final_norm",
    )(x, g)


def _pad_cols(x, width):
    return jnp.pad(x, ((0, 0), (0, width - x.shape[1])))


def _layer_weights(l, p):
    w_in = p["w_in"][l]
    c_z, c_x, c_dt = 3 * D_ATT, 3 * D_ATT + D_SSD, 3 * D_ATT + D_SSD + SSD_CONV_DIM
    c_glu = c_dt + H_SSD
    order = [w_in[:, c_z:c_z + D_SSD],
             w_in[:, c_x:c_x + D_SSD],
             w_in[:, c_glu:c_glu + 2 * C_CONF],
             w_in[:, c_x + D_SSD:c_dt],
             w_in[:, 0:3 * D_ATT]]
    w_up = p["w_up"][l]
    row = lambda v: v.reshape(1, -1)
    return dict(
        w_main=jnp.concatenate(order, axis=1).astype(BF16),
        w_dt=_pad_cols(w_in[:, c_dt:c_dt + H_SSD], LANES).astype(BF16),
        g_mix=row(p["norm_mix_g"][l]),
        att_bias=p["att_logit_bias"][l],
        att_g=row(p["att_norm_g"][l]),
        ssd_cw=p["ssd_conv_w"][l], ssd_cb=row(p["ssd_conv_b"][l]),
        dtb=_pad_cols(row(p["ssd_dt_bias"][l]), LANES), alog=_pad_cols(row(p["ssd_a_log"][l]), LANES),
        dd=row(jnp.repeat(p["ssd_d"][l], P_SSD)), ssd_ng=row(p["ssd_norm_g"][l]),
        conf_w=p["conf_conv_w"][l], conf_b=row(p["conf_conv_b"][l]),
        conf_lg=row(p["conf_ln_g"][l]), conf_lb=row(p["conf_ln_b"][l]),
        w_out=p["w_out"][l].astype(BF16), g_ffn=row(p["norm_ffn_g"][l]),
        w_gate=w_up[:, :D_FF].astype(BF16), w_val=w_up[:, D_FF:].astype(BF16),
        ffn_cw=p["ffn_conv_w"][l], ffn_cb=row(p["ffn_conv_b"][l]),
        w_down=p["w_down"][l].astype(BF16),
    )


def _xbc_cols(proj3):
    return jnp.concatenate([proj3[..., COL_XS:COL_XS + D_SSD], proj3[..., COL_BC:COL_BC + 2 * G_SSD * N_SSD]],
                           axis=-1)


def kernel(x_prompt, x_sample, cache_k, cache_v, state_ssm, state_ssd_conv, state_conf_conv, state_ffn_conv, page_table, norm_mix_g, w_in, att_logit_bias, att_norm_g, ssd_conv_w, ssd_conv_b, ssd_dt_bias, ssd_a_log, ssd_d, ssd_norm_g, conf_conv_w, conf_conv_b, conf_ln_g, conf_ln_b, w_out, norm_ffn_g, w_up, ffn_conv_w, ffn_conv_b, w_down, norm_final_g):
    params = dict(norm_mix_g=norm_mix_g, w_in=w_in, att_logit_bias=att_logit_bias, att_norm_g=att_norm_g,
                  ssd_conv_w=ssd_conv_w, ssd_conv_b=ssd_conv_b, ssd_dt_bias=ssd_dt_bias, ssd_a_log=ssd_a_log,
                  ssd_d=ssd_d, ssd_norm_g=ssd_norm_g, conf_conv_w=conf_conv_w, conf_conv_b=conf_conv_b,
                  conf_ln_g=conf_ln_g, conf_ln_b=conf_ln_b, w_out=w_out, norm_ffn_g=norm_ffn_g, w_up=w_up,
                  ffn_conv_w=ffn_conv_w, ffn_conv_b=ffn_conv_b, w_down=w_down)
    depth = w_in.shape[0]
    bp, sp, d = x_prompt.shape
    bs, ss, _ = x_sample.shape
    assert ss == SUBLANES and d == D_MODEL
    n_pool = cache_k.shape[1]
    xp = x_prompt.reshape(bp * sp, d)
    xs = x_sample.reshape(bs * ss, d)
    uo = _cumsum_matrix()
    tri = jnp.tril(jnp.ones((SSD_CHUNK, SSD_CHUNK), F32)).astype(BF16)
    zeros_ssd_conv = jnp.zeros((bp, SUBLANES, SSD_CONV_DIM), F32)
    zeros_conf_conv = jnp.zeros((bp, CONF_PAD, C_CONF), F32)
    outs_p, outs_s = [], []
    for l in range(depth):
        w = _layer_weights(l, params)

        proj, dtp = _in_proj(xp, w["g_mix"], w["w_main"], w["w_dt"])
        att, k_t, v_t = _att_prompt(proj, w["att_bias"], w["att_g"], uo, bp, sp)
        y, h_t = _ssd(proj, dtp, zeros_ssd_conv, None, w["ssd_cw"], w["ssd_cb"], w["dtb"], w["alog"], w["dd"],
                      w["ssd_ng"], tri, bp, sp, BF16)
        u, conf_tail = _conformer(proj, zeros_conf_conv, w["conf_w"], w["conf_b"], w["conf_lg"], w["conf_lb"],
                                  bp, sp, BF16)
        x1, h2 = _out_proj(xp, att, y, u, w["w_out"], w["g_ffn"])
        act, gate_tail = _ffn_up(h2, w["w_gate"], w["w_val"], w["ffn_cw"], w["ffn_cb"], sp)
        xp = _ffn_down(x1, act, w["w_down"])
        proj3 = proj.reshape(bp, sp, D_PROJ)
        tiles_per_seq = gate_tail.shape[0] // SUBLANES // bp
        gt = gate_tail.reshape(bp, tiles_per_seq, SUBLANES, D_FF)
        outs_p.append((
            k_t, v_t,
            h_t,
            _xbc_cols(proj3[:, sp - (SSD_CONV_W - 1):]),
            conf_tail[:, CONF_PAD - (CONF_W - 1):],
            gt[:, -1, SUBLANES - (FFN_CONV_W - 1):]))

        proj, dtp = _in_proj(xs, w["g_mix"], w["w_main"], w["w_dt"])
        bias_rows = jnp.broadcast_to(jnp.repeat(w["att_bias"], SUBLANES)[:, None], (H_ATT * SUBLANES, TK_ATT))
        att = _att_sample(proj, cache_k, cache_v, page_table, l, n_pool, bias_rows, w["att_g"], uo)
        conv0 = jnp.pad(state_ssd_conv[l], ((0, 0), (SUBLANES - (SSD_CONV_W - 1), 0), (0, 0)))
        y, h_t = _ssd(proj, dtp, conv0, state_ssm[l], w["ssd_cw"], w["ssd_cb"], w["dtb"],
                      w["alog"], w["dd"], w["ssd_ng"], tri, bs, ss, F32)
        conf0 = jnp.pad(state_conf_conv[l], ((0, 0), (CONF_PAD - (CONF_W - 1), 0), (0, 0)))
        u, conf_tail = _conformer(proj, conf0, w["conf_w"], w["conf_b"], w["conf_lg"], w["conf_lb"], bs, ss, F32)
        x1, h2 = _out_proj(xs, att, y, u, w["w_out"], w["g_ffn"])
        st = state_ffn_conv[l]
        s1 = jnp.concatenate([st[:, 1:2], jnp.zeros((bs, ss - 1, D_FF), F32)], axis=1).reshape(bs * ss, -1)
        s2 = jnp.concatenate([st, jnp.zeros((bs, ss - 2, D_FF), F32)], axis=1).reshape(bs * ss, -1)
        act, gate_full = _ffn_up(h2, w["w_gate"], w["w_val"], w["ffn_cw"], w["ffn_cb"], ss, s1, s2)
        xs = _ffn_down(x1, act, w["w_down"])
        proj3 = proj.reshape(bs, ss, D_PROJ)
        outs_s.append((
            proj3[..., COL_K:COL_K + D_ATT].reshape(bs, ss, H_ATT, D_HEAD_ATT),
            proj3[..., COL_V:COL_V + D_ATT].reshape(bs, ss, H_ATT, D_HEAD_ATT),
            h_t,
            _xbc_cols(proj3[:, ss - (SSD_CONV_W - 1):]),
            conf_tail[:, CONF_PAD - (CONF_W - 1):],
            gate_full.reshape(bs, ss, D_FF)[:, ss - (FFN_CONV_W - 1):]))

    g_final = norm_final_g.reshape(1, d)
    y_prompt = _final_norm(xp, g_final).reshape(bp, sp, d)
    y_sample = _final_norm(xs, g_final).reshape(bs, ss, d)
    stack = lambda outs, i: jnp.stack([o[i] for o in outs])

    def kv_prompt(i):
        t = stack(outs_p, i).reshape(depth, bp, H_ATT, D_HEAD_ATT, sp)
        return jnp.transpose(t, (0, 1, 4, 2, 3))

    return (y_prompt, y_sample,
            kv_prompt(0), kv_prompt(1), stack(outs_s, 0), stack(outs_s, 1),
            stack(outs_p, 2), stack(outs_s, 2), stack(outs_p, 3), stack(outs_s, 3),
            stack(outs_p, 4), stack(outs_s, 4), stack(outs_p, 5), stack(outs_s, 5))
```
